```python
import jax, jax.numpy as jnp
from jax import lax
import numpy as np

D_MODEL = 1024
BATCH = 4
SEQ = 8192
DEPTH = 4

MEM_LEN = 256
N_MIXERS = 2
CHUNK = 64
NORM_EPS = 1e-6
GLA_HEADS = 4
GLA_DK = (D_MODEL // 2) // GLA_HEADS
GLA_DV = D_MODEL // GLA_HEADS
GLA_GATE_RANK = 16
GLA_TAU = 16.0
HGRN_EXPAND = 128
HGRN_HEADS = D_MODEL // HGRN_EXPAND
HGRN_DF = HGRN_EXPAND
HGRN_FDIM = HGRN_HEADS * HGRN_DF
HGRN_DI = D_MODEL // HGRN_HEADS
XA_HEADS = 4
XA_DH = 128
XA_DIM = XA_HEADS * XA_DH
MLP_HIDDEN = 4 * D_MODEL
MIX_OUT_DIM = D_MODEL + XA_DIM
GLA_SPLITS = (GLA_HEADS * GLA_DK, GLA_HEADS * GLA_DK, GLA_HEADS * GLA_DV, GLA_GATE_RANK, D_MODEL, XA_DIM)
HGRN_SPLITS = (HGRN_FDIM, HGRN_FDIM, D_MODEL, D_MODEL, XA_DIM)
GLA_IN_DIM = sum(GLA_SPLITS)
HGRN_IN_DIM = sum(HGRN_SPLITS)

kernel_name = "hybrid_gla_hgrn2_memxattn_sqrelu"


def rms_norm(x, gain):
    x32 = x.astype(jnp.float32)
    y = x32 * lax.rsqrt(jnp.mean(x32 * x32, axis=-1, keepdims=True) + NORM_EPS)
    return (y * gain.astype(jnp.float32)).astype(x.dtype)


def split_cols(a, sizes):
    idx = [int(v) for v in np.cumsum(sizes)[:-1]]
    return jnp.split(a, idx, axis=-1)


def chunked_gated_linear_attention(q, k, v, log_decay):
    B, T, H, dk = q.shape
    dv = v.shape[-1]
    n = T // CHUNK

    def to_chunks(a):
        return a.astype(jnp.float32).reshape(B, n, CHUNK, H, a.shape[-1]).transpose(1, 0, 3, 2, 4)

    causal = jnp.tril(jnp.ones((CHUNK, CHUNK), dtype=bool))[None, None, :, :, None]

    def step(S, inp):
        qb, kb, vb, gb = inp
        b = jnp.cumsum(gb, axis=2)
        rel = jnp.exp(jnp.where(causal, b[:, :, :, None, :] - b[:, :, None, :, :], -jnp.inf))
        attn = jnp.einsum('bhtd,bhsd,bhtsd->bhts', qb, kb, rel)
        o = (jnp.einsum('bhtd,bhdv->bhtv', qb * jnp.exp(b), S)
             + jnp.einsum('bhts,bhsv->bhtv', attn, vb))
        b_last = b[:, :, -1, :]
        S = (jnp.exp(b_last)[..., None] * S
             + jnp.einsum('bhsd,bhsv->bhdv', kb * jnp.exp(b_last[:, :, None, :] - b), vb))
        return S, o

    S0 = jnp.zeros((B, H, dk, dv), jnp.float32)
    _, o = lax.scan(step, S0, (to_chunks(q), to_chunks(k), to_chunks(v), to_chunks(log_decay)))
    return o.transpose(1, 0, 3, 2, 4).reshape(B, T, H, dv)


def gla_branch(h, w_in, w_gate2, b_gate, out_gain):
    B, T, _ = h.shape
    q, k, v, g_lr, r, xq = split_cols(h @ w_in, GLA_SPLITS)
    q = q.reshape(B, T, GLA_HEADS, GLA_DK) * (GLA_DK ** -0.5)
    k = k.reshape(B, T, GLA_HEADS, GLA_DK)
    v = v.reshape(B, T, GLA_HEADS, GLA_DV)
    z = (g_lr @ w_gate2 + b_gate).astype(jnp.float32)
    log_alpha = (jax.nn.log_sigmoid(z) / GLA_TAU).reshape(B, T, GLA_HEADS, GLA_DK)
    o = chunked_gated_linear_attention(q, k, v, log_alpha)
    o = rms_norm(o, out_gain) * jax.nn.silu(r.astype(jnp.float32)).reshape(B, T, GLA_HEADS, GLA_DV)
    return o.reshape(B, T, D_MODEL).astype(h.dtype), xq


def hgrn2_branch(h, w_in, lower_bound, out_gain):
    B, T, _ = h.shape
    q, f, i, g, xq = split_cols(h @ w_in, HGRN_SPLITS)
    q = jax.nn.silu(q).reshape(B, T, HGRN_HEADS, HGRN_DF) * (HGRN_DF ** -0.5)
    lb = lower_bound.astype(jnp.float32)
    forget = lb + (1.0 - lb) * jax.nn.sigmoid(f.astype(jnp.float32))
    k = (1.0 - forget).reshape(B, T, HGRN_HEADS, HGRN_DF)
    log_f = jnp.log(forget).reshape(B, T, HGRN_HEADS, HGRN_DF)
    i = i.reshape(B, T, HGRN_HEADS, HGRN_DI)
    o = chunked_gated_linear_attention(q, k, i, log_f)
    o = rms_norm(o, out_gain) * jax.nn.silu(g.astype(jnp.float32)).reshape(B, T, HGRN_HEADS, HGRN_DI)
    return o.reshape(B, T, D_MODEL).astype(h.dtype), xq


def memory_cross_attention(xq, m, w_kv):
    B, T, _ = xq.shape
    km, vm = split_cols(m @ w_kv, (XA_DIM, XA_DIM))
    qh = xq.reshape(B, T, XA_HEADS, XA_DH)
    km = km.reshape(B, -1, XA_HEADS, XA_DH)
    vm = vm.reshape(B, -1, XA_HEADS, XA_DH)
    scores = jnp.einsum('bthd,bmhd->bhtm', qh, km).astype(jnp.float32) * (XA_DH ** -0.5)
    p = jax.nn.softmax(scores, axis=-1).astype(vm.dtype)
    out = jnp.einsum('bhtm,bmhd->bthd', p, vm)
    return out.reshape(B, T, XA_DIM)


def sq_relu_mlp(h, w_up, w_down):
    u = jax.nn.relu(h @ w_up)
    return (u * u) @ w_down


def setup_inputs(seed: int = 0) -> dict:
    key = jax.random.key(seed)
    ks = jax.random.split(key, 20)
    n_gla = len(range(0, DEPTH, N_MIXERS))
    n_hgrn = len(range(1, DEPTH, N_MIXERS))
    f32 = jnp.float32

    def w(k, shape, fan_in):
        return jax.random.normal(k, shape, f32) * (fan_in ** -0.5)

    def gain(k, shape):
        return 1.0 + 0.02 * jax.random.normal(k, shape, f32)

    return {
        "x": jax.random.normal(ks[0], (BATCH, SEQ, D_MODEL), f32),
        "mem": jax.random.normal(ks[1], (BATCH, MEM_LEN, D_MODEL), f32),
        "norm_mix": gain(ks[2], (DEPTH, D_MODEL)),
        "norm_mem": gain(ks[3], (DEPTH, D_MODEL)),
        "w_kv": w(ks[4], (DEPTH, D_MODEL, 2 * XA_DIM), D_MODEL),
        "w_out": w(ks[5], (DEPTH, MIX_OUT_DIM, D_MODEL), MIX_OUT_DIM),
        "norm_mlp": gain(ks[6], (DEPTH, D_MODEL)),
        "w_up": w(ks[7], (DEPTH, D_MODEL, MLP_HIDDEN), D_MODEL),
        "w_down": w(ks[8], (DEPTH, MLP_HIDDEN, D_MODEL), MLP_HIDDEN),
        "gla_w_in": w(ks[9], (n_gla, D_MODEL, GLA_IN_DIM), D_MODEL),
        "gla_w_gate2": w(ks[10], (n_gla, GLA_GATE_RANK, GLA_HEADS * GLA_DK), GLA_GATE_RANK),
        "gla_b_gate": 0.1 * jax.random.normal(ks[11], (n_gla, GLA_HEADS * GLA_DK), f32),
        "gla_out_gain": gain(ks[12], (n_gla, GLA_DV)),
        "hgrn_w_in": w(ks[13], (n_hgrn, D_MODEL, HGRN_IN_DIM), D_MODEL),
        "hgrn_lower_bounds": 0.1 * jax.random.normal(ks[14], (DEPTH, HGRN_FDIM), f32),
        "hgrn_out_gain": gain(ks[15], (n_hgrn, HGRN_DI)),
        "final_norm": gain(ks[16], (D_MODEL,)),
    }


def reference(x, mem, norm_mix, norm_mem, w_kv, w_out, norm_mlp, w_up, w_down,
              gla_w_in, gla_w_gate2, gla_b_gate, gla_out_gain,
              hgrn_w_in, hgrn_lower_bounds, hgrn_out_gain, final_norm):
    p = jax.nn.softmax(hgrn_lower_bounds.astype(jnp.float32), axis=0)
    lower_bounds = jnp.cumsum(p, axis=0) - p[0]

    for i in range(DEPTH):
        j = i // N_MIXERS
        h = rms_norm(x, norm_mix[i])
        m = rms_norm(mem, norm_mem[i])
        if i % N_MIXERS == 0:
            y_mix, xq = gla_branch(h, gla_w_in[j], gla_w_gate2[j], gla_b_gate[j], gla_out_gain[j])
        else:
            y_mix, xq = hgrn2_branch(h, hgrn_w_in[j], lower_bounds[i], hgrn_out_gain[j])
        y_mem = memory_cross_attention(xq, m, w_kv[i])
        x = x + jnp.concatenate([y_mix, y_mem.astype(y_mix.dtype)], axis=-1) @ w_out[i]
        x = x + sq_relu_mlp(rms_norm(x, norm_mlp[i]), w_up[i], w_down[i])
    return rms_norm(x, final_norm)
```

```python
import functools

import jax
import jax.numpy as jnp
from jax import lax
from jax.experimental import pallas as pl
from jax.experimental.pallas import tpu as pltpu

F32 = jnp.float32
BF16 = jnp.bfloat16

D_MODEL = 1024
DEPTH = 4
MEM_LEN = 256
NORM_EPS = 1e-6
GLA_HEADS = 4
GLA_DK = 128
GLA_DV = 256
GLA_GATE_RANK = 16
GLA_TAU = 16.0
HGRN_HEADS = 8
HGRN_DF = 128
HGRN_DI = 128
XA_HEADS = 4
XA_DH = 128
XA_DIM = XA_HEADS * XA_DH
MLP_HIDDEN = 4 * D_MODEL

LANES = 128
V7X_VMEM_BYTES = 64 * 1024 * 1024
VMEM_LIMIT_BYTES = V7X_VMEM_BYTES - 8 * 1024 * 1024

CHUNK = 64
ROW_TILE = 512
MIX_TILE = 512
MLP_HIDDEN_TILE = 1024
GATE_PAD = LANES


def _params(semantics):
    return pltpu.CompilerParams(dimension_semantics=semantics, vmem_limit_bytes=VMEM_LIMIT_BYTES)


def _resident(shape):
    zeros = (0,) * len(shape)
    return pl.BlockSpec(shape, lambda *_: zeros, pipeline_mode=pl.Buffered(1))


def _rms_normed(x, gain):
    ms = jnp.mean(x * x, axis=-1, keepdims=True)
    return x * lax.rsqrt(ms + NORM_EPS) * gain


def _sigmoid(x):
    return 1.0 / (1.0 + jnp.exp(-x))


def _memkv_kernel(mem_ref, gain_ref, w_ref, o_ref):
    h = _rms_normed(mem_ref[...], gain_ref[0]).astype(BF16)
    o_ref[0] = jnp.dot(h, w_ref[0], preferred_element_type=F32).astype(BF16)


def _memkv(mem2d, norm_mem, w_kv):
    rows = mem2d.shape[0]
    return pl.pallas_call(
        _memkv_kernel,
        grid=(DEPTH,),
        in_specs=[
            _resident((rows, D_MODEL)),
            pl.BlockSpec((1, 1, D_MODEL), lambda l: (l, 0, 0)),
            pl.BlockSpec((1, D_MODEL, 2 * XA_DIM), lambda l: (l, 0, 0)),
        ],
        out_specs=pl.BlockSpec((1, rows, 2 * XA_DIM), lambda l: (l, 0, 0)),
        out_shape=jax.ShapeDtypeStruct((DEPTH, rows, 2 * XA_DIM), BF16),
        compiler_params=_params(("arbitrary",)),
        name="memkv",
    )(mem2d, norm_mem.reshape(DEPTH, 1, D_MODEL), w_kv)


_GLA_Q = (0, 512)
_GLA_K = (512, 1024)
_GLA_V = (1024, 2048)
_GLA_G = (2048, 2048 + GATE_PAD)
_GLA_R = (_GLA_G[1], _GLA_G[1] + 1024)
_GLA_XQ = (_GLA_R[1], _GLA_R[1] + XA_DIM)
_GLA_COLS = _GLA_XQ[1]


def _proj_gla_kernel(x_ref, gain_ref, w_ref, wg2_ref, bg_ref,
                     q_ref, k_ref, v_ref, ga_ref, r_ref, xq_ref):
    h = _rms_normed(x_ref[...], gain_ref[...]).astype(BF16)

    def mm(cols):
        return jnp.dot(h, w_ref[:, cols[0]:cols[1]], preferred_element_type=F32)

    q_ref[...] = (mm(_GLA_Q) * (GLA_DK ** -0.5)).astype(BF16)
    k_ref[...] = mm(_GLA_K).astype(BF16)
    v_ref[...] = mm(_GLA_V).astype(BF16)
    g_lr = mm(_GLA_G).astype(BF16)
    z = jnp.dot(g_lr, wg2_ref[...], preferred_element_type=F32) + bg_ref[...]
    ga_ref[...] = (jnp.minimum(z, 0.0) - jnp.log(1.0 + jnp.exp(-jnp.abs(z)))) * (1.0 / GLA_TAU)
    r_ref[...] = mm(_GLA_R).astype(BF16)
    xq_ref[...] = mm(_GLA_XQ).astype(BF16)


def _proj_gla(x2d, gain, w_in, w_gate2, b_gate):
    n = x2d.shape[0]
    split = [0, 512, 1024, 2048, 2048 + GLA_GATE_RANK, 3088, 3600]
    parts = [w_in[:, a:b] for a, b in zip(split[:-1], split[1:])]
    parts[3] = jnp.pad(parts[3], ((0, 0), (0, GATE_PAD - GLA_GATE_RANK)))
    w = jnp.concatenate(parts, axis=1).astype(BF16)
    wg2 = jnp.pad(w_gate2, ((0, GATE_PAD - GLA_GATE_RANK), (0, 0))).astype(BF16)
    row = lambda width: pl.BlockSpec((ROW_TILE, width), lambda i: (i, 0))
    out = lambda width, dt: jax.ShapeDtypeStruct((n, width), dt)
    return pl.pallas_call(
        _proj_gla_kernel,
        grid=(n // ROW_TILE,),
        in_specs=[row(D_MODEL), _resident((1, D_MODEL)), _resident((D_MODEL, _GLA_COLS)),
                  _resident((GATE_PAD, 512)), _resident((1, 512))],
        out_specs=[row(512), row(512), row(1024), row(512), row(1024), row(XA_DIM)],
        out_shape=[out(512, BF16), out(512, BF16), out(1024, BF16), out(512, F32),
                   out(1024, BF16), out(XA_DIM, BF16)],
        compiler_params=_params(("parallel",)),
        name="proj_gla",
    )(x2d, gain.reshape(1, D_MODEL), w, wg2, b_gate.reshape(1, 512))


def _proj_hgrn_kernel(x_ref, gain_ref, w_ref, lbraw_ref,
                      q_ref, k_ref, v_ref, ga_ref, g_ref, xq_ref, *, layer):
    h = _rms_normed(x_ref[...], gain_ref[...]).astype(BF16)

    def mm(a, b):
        return jnp.dot(h, w_ref[:, a:b], preferred_element_type=F32)

    raw = lbraw_ref[...]
    e = jnp.exp(raw - jnp.max(raw, axis=0, keepdims=True))
    p = e / jnp.sum(e, axis=0, keepdims=True)
    lb = jnp.sum(p[1:layer + 1, :], axis=0, keepdims=True)

    qf = mm(0, 1024)
    q_ref[...] = (qf * _sigmoid(qf) * (HGRN_DF ** -0.5)).astype(BF16)
    forget = lb + (1.0 - lb) * _sigmoid(mm(1024, 2048))
    k_ref[...] = (1.0 - forget).astype(BF16)
    ga_ref[...] = jnp.log(forget)
    v_ref[...] = mm(2048, 3072).astype(BF16)
    g_ref[...] = mm(3072, 4096).astype(BF16)
    xq_ref[...] = mm(4096, 4608).astype(BF16)


def _proj_hgrn(x2d, gain, w_in, lower_bounds_raw, layer):
    n = x2d.shape[0]
    row = lambda width: pl.BlockSpec((ROW_TILE, width), lambda i: (i, 0))
    out = lambda width, dt: jax.ShapeDtypeStruct((n, width), dt)
    return pl.pallas_call(
        functools.partial(_proj_hgrn_kernel, layer=layer),
        grid=(n // ROW_TILE,),
        in_specs=[row(D_MODEL), _resident((1, D_MODEL)), _resident((D_MODEL, 4608)),
                  _resident((DEPTH, 1024))],
        out_specs=[row(1024), row(1024), row(1024), row(1024), row(1024), row(XA_DIM)],
        out_shape=[out(1024, BF16), out(1024, BF16), out(1024, BF16), out(1024, F32),
                   out(1024, BF16), out(XA_DIM, BF16)],
        compiler_params=_params(("parallel",)),
        name="proj_hgrn",
    )(x2d, gain.reshape(1, D_MODEL), w_in.astype(BF16), lower_bounds_raw)


_NT = (((1,), (1,)), ((), ()))
_TN = (((0,), (0,)), ((), ()))


def _row_ref(b, n):
    if n >= 4:
        pieces = [jnp.broadcast_to(b[s + n - 1:s + n, :], (2 * n, LANES))
                  for s in range(0, CHUNK, 2 * n)]
        return jnp.concatenate(pieces, axis=0)
    i = lax.broadcasted_iota(jnp.int32, b.shape, 0) & (2 * n - 1)
    if n == 1:
        return jnp.where(i == 1, pltpu.roll(b, 1, 0), b)
    r = jnp.where(i == 0, pltpu.roll(b, CHUNK - 1, 0), b)
    r = jnp.where(i == 2, pltpu.roll(b, 1, 0), r)
    return jnp.where(i == 3, pltpu.roll(b, 2, 0), r)


def _mixer_kernel(q_ref, k_ref, v_ref, ga_ref, gate_ref, gain_ref, o_ref, s_ref, *, dv):
    @pl.when(pl.program_id(2) == 0)
    def _():
        s_ref[...] = jnp.zeros_like(s_ref)

    rows = lax.broadcasted_iota(jnp.int32, (CHUNK, CHUNK), 0)
    cols = lax.broadcasted_iota(jnp.int32, (CHUNK, CHUNK), 1)
    tri = (rows >= cols).astype(BF16)
    sep = rows ^ cols
    trow = lax.broadcasted_iota(jnp.int32, (CHUNK, LANES), 0)
    gain = gain_ref[...]

    def chunk(c, carry):
        sl = pl.ds(pl.multiple_of(c * CHUNK, CHUNK), CHUNK)
        g = ga_ref[sl, :]
        g_hi = g.astype(BF16)
        g_lo = (g - g_hi.astype(F32)).astype(BF16)
        bb = jnp.dot(tri, jnp.concatenate([g_hi, g_lo], axis=1), preferred_element_type=F32)
        b = bb[:, :LANES] + bb[:, LANES:]
        qb = q_ref[sl, :]
        kb = k_ref[sl, :]
        v = v_ref[sl, :]
        q = qb.astype(F32)
        k = kb.astype(F32)

        attn = lax.dot_general(qb, kb, _NT, preferred_element_type=F32)
        for n in (1, 2, 4, 8, 16, 32):
            upper = (trow & n) != 0
            d = b - _row_ref(b, n)
            w = jnp.exp(jnp.where(upper, d, -d))
            qt = (q * jnp.where(upper, w, 0.0)).astype(BF16)
            kt = (k * jnp.where(upper, 0.0, w)).astype(BF16)
            p = lax.dot_general(qt, kt, _NT, preferred_element_type=F32)
            attn = jnp.where(sep >= n, p, attn)

        s_old = s_ref[...]
        b_last = b[CHUNK - 1:CHUNK, :]
        qd = (q * jnp.exp(b)).astype(BF16)
        o = (jnp.dot(attn.astype(BF16), v, preferred_element_type=F32)
             + jnp.dot(qd, s_old.astype(BF16), preferred_element_type=F32))

        kd = (k * jnp.exp(b_last - b)).astype(BF16)
        decay = jnp.transpose(jnp.broadcast_to(jnp.exp(b_last), (LANES, LANES)))
        if dv > LANES:
            decay = jnp.concatenate([decay] * (dv // LANES), axis=1)
        s_ref[...] = decay * s_old + lax.dot_general(kd, v, _TN, preferred_element_type=F32)

        gate = gate_ref[sl, :].astype(F32)
        y = _rms_normed(o, gain) * (gate * _sigmoid(gate))
        o_ref[sl, :] = y.astype(o_ref.dtype)
        return carry

    lax.fori_loop(0, MIX_TILE // CHUNK, chunk, 0)


def _mixer(q, k, v, ga, gate, out_gain, *, batch, heads, dv):
    n = q.shape[0]
    tiles = n // batch // MIX_TILE
    kspec = pl.BlockSpec((MIX_TILE, LANES), lambda b, h, t: (b * tiles + t, h))
    vspec = pl.BlockSpec((MIX_TILE, dv), lambda b, h, t: (b * tiles + t, h))
    return pl.pallas_call(
        functools.partial(_mixer_kernel, dv=dv),
        grid=(batch, heads, tiles),
        in_specs=[kspec, kspec, vspec, kspec, vspec, _resident((1, dv))],
        out_specs=vspec,
        out_shape=jax.ShapeDtypeStruct((n, D_MODEL), BF16),
        scratch_shapes=[pltpu.VMEM((LANES, dv), F32)],
        compiler_params=_params(("parallel", "parallel", "arbitrary")),
        name=f"mixer_dv{dv}",
    )(q, k, v, ga, gate, out_gain.reshape(1, dv))


def _attn_out_kernel(x_ref, ymix_ref, xq_ref, kv_ref, wout_ref, o_ref):
    acc = x_ref[...] + jnp.dot(ymix_ref[...], wout_ref[0:D_MODEL, :], preferred_element_type=F32)
    xq = xq_ref[...]
    kv = kv_ref[0]
    heads = []
    for h in range(XA_HEADS):
        lo, hi = h * XA_DH, (h + 1) * XA_DH
        s = lax.dot_general(xq[:, lo:hi], kv[:, lo:hi], _NT, preferred_element_type=F32)
        s = s * (XA_DH ** -0.5)
        e = jnp.exp(s - jnp.max(s, axis=-1, keepdims=True)).astype(BF16)
        vh = kv[:, XA_DIM + lo:XA_DIM + hi]
        num_den = jnp.dot(e, jnp.concatenate([vh, jnp.ones_like(vh)], axis=1),
                          preferred_element_type=F32)
        heads.append((num_den[:, :XA_DH] / num_den[:, XA_DH:]).astype(BF16))
    y_mem = jnp.concatenate(heads, axis=1)
    o_ref[...] = acc + jnp.dot(y_mem, wout_ref[D_MODEL:, :], preferred_element_type=F32)


def _attn_out(x2d, y_mix, xq, kv, w_out, layer, *, batch):
    n = x2d.shape[0]
    tiles = n // batch // ROW_TILE
    row = lambda width: pl.BlockSpec((ROW_TILE, width), lambda i: (i, 0))
    return pl.pallas_call(
        _attn_out_kernel,
        grid=(n // ROW_TILE,),
        in_specs=[row(D_MODEL), row(D_MODEL), row(XA_DIM),
                  pl.BlockSpec((1, MEM_LEN, 2 * XA_DIM), lambda i: (layer * batch + i // tiles, 0, 0)),
                  _resident((D_MODEL + XA_DIM, D_MODEL))],
        out_specs=row(D_MODEL),
        out_shape=jax.ShapeDtypeStruct((n, D_MODEL), F32),
        compiler_params=_params(("parallel",)),
        name="attn_out",
    )(x2d, y_mix, xq, kv, w_out.astype(BF16))


def _mlp_kernel(x_ref, gain_ref, wup_ref, wdn_ref, fgain_ref, o_ref, *, final):
    x = x_ref[...]
    h = _rms_normed(x, gain_ref[...]).astype(BF16)
    acc = x
    for c in range(0, MLP_HIDDEN, MLP_HIDDEN_TILE):
        u = jnp.maximum(jnp.dot(h, wup_ref[:, c:c + MLP_HIDDEN_TILE], preferred_element_type=F32), 0.0)
        acc = acc + jnp.dot((u * u).astype(BF16), wdn_ref[c:c + MLP_HIDDEN_TILE, :],
                            preferred_element_type=F32)
    o_ref[...] = _rms_normed(acc, fgain_ref[...]) if final else acc


def _mlp(x2d, gain, w_up, w_down, final_gain, final):
    n = x2d.shape[0]
    row = pl.BlockSpec((ROW_TILE, D_MODEL), lambda i: (i, 0))
    return pl.pallas_call(
        functools.partial(_mlp_kernel, final=final),
        grid=(n // ROW_TILE,),
        in_specs=[row, _resident((1, D_MODEL)), _resident((D_MODEL, MLP_HIDDEN)),
                  _resident((MLP_HIDDEN, D_MODEL)), _resident((1, D_MODEL))],
        out_specs=row,
        out_shape=jax.ShapeDtypeStruct((n, D_MODEL), F32),
        compiler_params=_params(("parallel",)),
        name="mlp_final" if final else "mlp",
    )(x2d, gain.reshape(1, D_MODEL), w_up.astype(BF16), w_down.astype(BF16),
      final_gain.reshape(1, D_MODEL))


def kernel(x, mem, norm_mix, norm_mem, w_kv, w_out, norm_mlp, w_up, w_down,
           gla_w_in, gla_w_gate2, gla_b_gate, gla_out_gain,
           hgrn_w_in, hgrn_lower_bounds, hgrn_out_gain, final_norm):
    batch, seq, _ = x.shape
    assert seq % MIX_TILE == 0 and (batch * seq) % ROW_TILE == 0 and seq % ROW_TILE == 0
    x2d = x.reshape(batch * seq, D_MODEL)
    kv = _memkv(mem.reshape(batch * MEM_LEN, D_MODEL), norm_mem, w_kv.astype(BF16))
    kv = kv.reshape(DEPTH * batch, MEM_LEN, 2 * XA_DIM)

    for i in range(DEPTH):
        j = i // 2
        if i % 2 == 0:
            q, k, v, ga, gate, xq = _proj_gla(x2d, norm_mix[i], gla_w_in[j], gla_w_gate2[j],
                                              gla_b_gate[j])
            y_mix = _mixer(q, k, v, ga, gate, gla_out_gain[j],
                           batch=batch, heads=GLA_HEADS, dv=GLA_DV)
        else:
            q, k, v, ga, gate, xq = _proj_hgrn(x2d, norm_mix[i], hgrn_w_in[j],
                                               hgrn_lower_bounds, i)
            y_mix = _mixer(q, k, v, ga, gate, hgrn_out_gain[j],
                           batch=batch, heads=HGRN_HEADS, dv=HGRN_DI)
        x2d = _attn_out(x2d, y_mix, xq, kv, w_out[i], i, batch=batch)
        x2d = _mlp(x2d, norm_mlp[i], w_up[i], w_down[i], final_norm, final=(i == DEPTH - 1))
    return x2d.reshape(batch, seq, D_MODEL)
```

```python
import functools

import jax
import jax.numpy as jnp
from jax import lax
from jax.experimental import pallas as pl
from jax.experimental.pallas import tpu as pltpu

F32 = jnp.float32
BF16 = jnp.bfloat16

D_MODEL = 1024
DEPTH = 4
MEM_LEN = 256
NORM_EPS = 1e-6
GLA_HEADS = 4
GLA_DK = 128
GLA_DV = 256
GLA_GATE_RANK = 16
GLA_TAU = 16.0
HGRN_HEADS = 8
HGRN_DF = 128
HGRN_DI = 128
XA_HEADS = 4
XA_DH = 128
XA_DIM = XA_HEADS * XA_DH
MLP_HIDDEN = 4 * D_MODEL

LANES = 128
V7X_VMEM_BYTES = 64 * 1024 * 1024
VMEM_LIMIT_BYTES = V7X_VMEM_BYTES - 8 * 1024 * 1024

CHUNK = 64
ROW_TILE = 512
MIX_TILE = 512
MLP_HIDDEN_TILE = 1024
GATE_PAD = LANES


def _params(semantics):
    return pltpu.CompilerParams(dimension_semantics=semantics, vmem_limit_bytes=VMEM_LIMIT_BYTES)


def _resident(shape):
    zeros = (0,) * len(shape)
    return pl.BlockSpec(shape, lambda *_: zeros, pipeline_mode=pl.Buffered(1))


def _rms_normed(x, gain):
    ms = jnp.mean(x * x, axis=-1, keepdims=True)
    return x * lax.rsqrt(ms + NORM_EPS) * gain


def _sigmoid(x):
    return 1.0 / (1.0 + jnp.exp(-x))


def _memkv_kernel(mem_ref, gain_ref, w_ref, o_ref):
    h = _rms_normed(mem_ref[...], gain_ref[0]).astype(BF16)
    o_ref[0] = jnp.dot(h, w_ref[0], preferred_element_type=F32).astype(BF16)


def _memkv(mem2d, norm_mem, w_kv):
    rows = mem2d.shape[0]
    return pl.pallas_call(
        _memkv_kernel,
        grid=(DEPTH,),
        in_specs=[
            _resident((rows, D_MODEL)),
            pl.BlockSpec((1, 1, D_MODEL), lambda l: (l, 0, 0)),
            pl.BlockSpec((1, D_MODEL, 2 * XA_DIM), lambda l: (l, 0, 0)),
        ],
        out_specs=pl.BlockSpec((1, rows, 2 * XA_DIM), lambda l: (l, 0, 0)),
        out_shape=jax.ShapeDtypeStruct((DEPTH, rows, 2 * XA_DIM), BF16),
        compiler_params=_params(("arbitrary",)),
        name="memkv",
    )(mem2d, norm_mem.reshape(DEPTH, 1, D_MODEL), w_kv)


_GLA_Q = (0, 512)
_GLA_K = (512, 1024)
_GLA_V = (1024, 2048)
_GLA_G = (2048, 2048 + GATE_PAD)
_GLA_R = (_GLA_G[1], _GLA_G[1] + 1024)
_GLA_XQ = (_GLA_R[1], _GLA_R[1] + XA_DIM)
_GLA_COLS = _GLA_XQ[1]


def _proj_gla_kernel(x_ref, gain_ref, w_ref, wg2_ref, bg_ref,
                     q_ref, k_ref, v_ref, ga_ref, r_ref, xq_ref):
    h = _rms_normed(x_ref[...], gain_ref[...]).astype(BF16)

    def mm(cols):
        return jnp.dot(h, w_ref[:, cols[0]:cols[1]], preferred_element_type=F32)

    q_ref[...] = (mm(_GLA_Q) * (GLA_DK ** -0.5)).astype(BF16)
    k_ref[...] = mm(_GLA_K).astype(BF16)
    v_ref[...] = mm(_GLA_V).astype(BF16)
    g_lr = mm(_GLA_G).astype(BF16)
    z = jnp.dot(g_lr, wg2_ref[...], preferred_element_type=F32) + bg_ref[...]
    ga_ref[...] = (jnp.minimum(z, 0.0) - jnp.log(1.0 + jnp.exp(-jnp.abs(z)))) * (1.0 / GLA_TAU)
    r_ref[...] = mm(_GLA_R).astype(BF16)
    xq_ref[...] = mm(_GLA_XQ).astype(BF16)


def _proj_gla(x2d, gain, w_in, w_gate2, b_gate):
    n = x2d.shape[0]
    split = [0, 512, 1024, 2048, 2048 + GLA_GATE_RANK, 3088, 3600]
    parts = [w_in[:, a:b] for a, b in zip(split[:-1], split[1:])]
    parts[3] = jnp.pad(parts[3], ((0, 0), (0, GATE_PAD - GLA_GATE_RANK)))
    w = jnp.concatenate(parts, axis=1).astype(BF16)
    wg2 = jnp.pad(w_gate2, ((0, GATE_PAD - GLA_GATE_RANK), (0, 0))).astype(BF16)
    row = lambda width: pl.BlockSpec((ROW_TILE, width), lambda i: (i, 0))
    out = lambda width, dt: jax.ShapeDtypeStruct((n, width), dt)
    return pl.pallas_call(
        _proj_gla_kernel,
        grid=(n // ROW_TILE,),
        in_specs=[row(D_MODEL), _resident((1, D_MODEL)), _resident((D_MODEL, _GLA_COLS)),
                  _resident((GATE_PAD, 512)), _resident((1, 512))],
        out_specs=[row(512), row(512), row(1024), row(512), row(1024), row(XA_DIM)],
        out_shape=[out(512, BF16), out(512, BF16), out(1024, BF16), out(512, F32),
                   out(1024, BF16), out(XA_DIM, BF16)],
        compiler_params=_params(("parallel",)),
        name="proj_gla",
    )(x2d, gain.reshape(1, D_MODEL), w, wg2, b_gate.reshape(1, 512))


def _proj_hgrn_kernel(x_ref, gain_ref, w_ref, lbraw_ref,
                      q_ref, k_ref, v_ref, ga_ref, g_ref, xq_ref, *, layer):
    h = _rms_normed(x_ref[...], gain_ref[...]).astype(BF16)

    def mm(a, b):
        return jnp.dot(h, w_ref[:, a:b], preferred_element_type=F32)

    raw = lbraw_ref[...]
    e = jnp.exp(raw - jnp.max(raw, axis=0, keepdims=True))
    p = e / jnp.sum(e, axis=0, keepdims=True)
    lb = jnp.sum(p[1:layer + 1, :], axis=0, keepdims=True)

    qf = mm(0, 1024)
    q_ref[...] = (qf * _sigmoid(qf) * (HGRN_DF ** -0.5)).astype(BF16)
    forget = lb + (1.0 - lb) * _sigmoid(mm(1024, 2048))
    k_ref[...] = (1.0 - forget).astype(BF16)
    ga_ref[...] = jnp.log(forget)
    v_ref[...] = mm(2048, 3072).astype(BF16)
    g_ref[...] = mm(3072, 4096).astype(BF16)
    xq_ref[...] = mm(4096, 4608).astype(BF16)


def _proj_hgrn(x2d, gain, w_in, lower_bounds_raw, layer):
    n = x2d.shape[0]
    row = lambda width: pl.BlockSpec((ROW_TILE, width), lambda i: (i, 0))
    out = lambda width, dt: jax.ShapeDtypeStruct((n, width), dt)
    return pl.pallas_call(
        functools.partial(_proj_hgrn_kernel, layer=layer),
        grid=(n // ROW_TILE,),
        in_specs=[row(D_MODEL), _resident((1, D_MODEL)), _resident((D_MODEL, 4608)),
                  _resident((DEPTH, 1024))],
        out_specs=[row(1024), row(1024), row(1024), row(1024), row(1024), row(XA_DIM)],
        out_shape=[out(1024, BF16), out(1024, BF16), out(1024, BF16), out(1024, F32),
                   out(1024, BF16), out(XA_DIM, BF16)],
        compiler_params=_params(("parallel",)),
        name="proj_hgrn",
    )(x2d, gain.reshape(1, D_MODEL), w_in.astype(BF16), lower_bounds_raw)


_NT = (((1,), (1,)), ((), ()))
_BNT = (((2,), (2,)), ((0,), (0,)))
_BNN = (((2,), (1,)), ((0,), (0,)))
_BTN = (((1,), (1,)), ((0,), (0,)))

LOG2E = 1.4426950408889634
N_CHUNKS = MIX_TILE // CHUNK
LEVELS = (1, 2, 4, 8, 16, 32)


def _row_ref(b, n):
    if n >= 4:
        blocks = b.reshape(MIX_TILE // (2 * n), 2 * n, LANES)
        return jnp.broadcast_to(blocks[:, n - 1:n, :], blocks.shape).reshape(MIX_TILE, LANES)
    i = lax.broadcasted_iota(jnp.int32, b.shape, 0) & (2 * n - 1)
    if n == 1:
        return jnp.where(i == 1, pltpu.roll(b, 1, 0), b)
    r = jnp.where(i == 0, pltpu.roll(b, MIX_TILE - 1, 0), b)
    r = jnp.where(i == 2, pltpu.roll(b, 1, 0), r)
    return jnp.where(i == 3, pltpu.roll(b, 2, 0), r)


def _neg_abs(x):
    bits = pltpu.bitcast(x, jnp.uint32) | jnp.uint32(0x80000000)
    return pltpu.bitcast(bits, F32)


def _mixer_kernel(q_ref, k_ref, v_ref, ga_ref, gate_ref, gain_ref, o_ref, s_ref, sall_ref, *, dv):
    @pl.when(pl.program_id(2) == 0)
    def _():
        s_ref[...] = jnp.zeros_like(s_ref)

    shape3 = (N_CHUNKS, CHUNK, CHUNK)
    rows = lax.broadcasted_iota(jnp.int32, shape3, 1)
    cols = lax.broadcasted_iota(jnp.int32, shape3, 2)
    tri = (rows >= cols).astype(BF16)
    sep = rows ^ cols
    trow = lax.broadcasted_iota(jnp.int32, (MIX_TILE, LANES), 0)

    def chunks(a):
        return a.reshape(N_CHUNKS, CHUNK, a.shape[-1])

    g = ga_ref[...]
    g_hi = g.astype(BF16)
    g_lo = (g - g_hi.astype(F32)).astype(BF16)
    bb = lax.dot_general(tri, chunks(jnp.concatenate([g_hi, g_lo], axis=1)), _BNN,
                         preferred_element_type=F32)
    b = ((bb[:, :, :LANES] + bb[:, :, LANES:]) * LOG2E).reshape(MIX_TILE, LANES)

    qb = q_ref[...]
    kb = k_ref[...]
    v = chunks(v_ref[...])
    q = qb.astype(F32)
    k = kb.astype(F32)

    attn = lax.dot_general(chunks(qb), chunks(kb), _BNT, preferred_element_type=F32)
    for n in LEVELS:
        w = jnp.exp2(_neg_abs(b - _row_ref(b, n)))
        x = chunks((jnp.where((trow & n) != 0, q, k) * w).astype(BF16))
        p = lax.dot_general(x, x, _BNT, preferred_element_type=F32)
        attn = jnp.where(sep >= n, p, attn)
    attn = jnp.where(rows >= cols, attn, 0.0).astype(BF16)

    b3 = chunks(b)
    b_last = b3[:, CHUNK - 1:CHUNK, :]
    qd = (chunks(q) * jnp.exp2(b3)).astype(BF16)
    kd = (chunks(k) * jnp.exp2(b_last - b3)).astype(BF16)
    u = lax.dot_general(kd, v, _BTN, preferred_element_type=F32)
    decay = jnp.swapaxes(jnp.broadcast_to(jnp.exp2(b_last), (N_CHUNKS, LANES, LANES)), 1, 2)
    if dv > LANES:
        decay = jnp.concatenate([decay] * (dv // LANES), axis=2)

    s = s_ref[...]
    for c in range(N_CHUNKS):
        sall_ref[c] = s.astype(BF16)
        s = decay[c] * s + u[c]
    s_ref[...] = s

    o = (lax.dot_general(attn, v, _BNN, preferred_element_type=F32)
         + lax.dot_general(qd, sall_ref[...], _BNN, preferred_element_type=F32))
    gate = gate_ref[...].astype(F32)
    y = _rms_normed(o.reshape(MIX_TILE, dv), gain_ref[...]) * (gate * _sigmoid(gate))
    o_ref[...] = y.astype(o_ref.dtype)


def _mixer(q, k, v, ga, gate, out_gain, *, batch, heads, dv):
    n = q.shape[0]
    tiles = n // batch // MIX_TILE
    kspec = pl.BlockSpec((MIX_TILE, LANES), lambda b, h, t: (b * tiles + t, h))
    vspec = pl.BlockSpec((MIX_TILE, dv), lambda b, h, t: (b * tiles + t, h))
    return pl.pallas_call(
        functools.partial(_mixer_kernel, dv=dv),
        grid=(batch, heads, tiles),
        in_specs=[kspec, kspec, vspec, kspec, vspec, _resident((1, dv))],
        out_specs=vspec,
        out_shape=jax.ShapeDtypeStruct((n, D_MODEL), BF16),
        scratch_shapes=[pltpu.VMEM((LANES, dv), F32), pltpu.VMEM((N_CHUNKS, LANES, dv), BF16)],
        compiler_params=_params(("parallel", "parallel", "arbitrary")),
        name=f"mixer_dv{dv}",
    )(q, k, v, ga, gate, out_gain.reshape(1, dv))


def _attn_out_kernel(x_ref, ymix_ref, xq_ref, kv_ref, wout_ref, o_ref):
    acc = x_ref[...] + jnp.dot(ymix_ref[...], wout_ref[0:D_MODEL, :], preferred_element_type=F32)
    xq = xq_ref[...]
    kv = kv_ref[0]
    heads = []
    for h in range(XA_HEADS):
        lo, hi = h * XA_DH, (h + 1) * XA_DH
        s = lax.dot_general(xq[:, lo:hi], kv[:, lo:hi], _NT, preferred_element_type=F32)
        s = s * (XA_DH ** -0.5)
        e = jnp.exp(s - jnp.max(s, axis=-1, keepdims=True)).astype(BF16)
        vh = kv[:, XA_DIM + lo:XA_DIM + hi]
        num_den = jnp.dot(e, jnp.concatenate([vh, jnp.ones_like(vh)], axis=1),
                          preferred_element_type=F32)
        heads.append((num_den[:, :XA_DH] / num_den[:, XA_DH:]).astype(BF16))
    y_mem = jnp.concatenate(heads, axis=1)
    o_ref[...] = acc + jnp.dot(y_mem, wout_ref[D_MODEL:, :], preferred_element_type=F32)


def _attn_out(x2d, y_mix, xq, kv, w_out, layer, *, batch):
    n = x2d.shape[0]
    tiles = n // batch // ROW_TILE
    row = lambda width: pl.BlockSpec((ROW_TILE, width), lambda i: (i, 0))
    return pl.pallas_call(
        _attn_out_kernel,
        grid=(n // ROW_TILE,),
        in_specs=[row(D_MODEL), row(D_MODEL), row(XA_DIM),
                  pl.BlockSpec((1, MEM_LEN, 2 * XA_DIM), lambda i: (layer * batch + i // tiles, 0, 0)),
                  _resident((D_MODEL + XA_DIM, D_MODEL))],
        out_specs=row(D_MODEL),
        out_shape=jax.ShapeDtypeStruct((n, D_MODEL), F32),
        compiler_params=_params(("parallel",)),
        name="attn_out",
    )(x2d, y_mix, xq, kv, w_out.astype(BF16))


def _mlp_kernel(x_ref, gain_ref, wup_ref, wdn_ref, fgain_ref, o_ref, *, final):
    x = x_ref[...]
    h = _rms_normed(x, gain_ref[...]).astype(BF16)
    acc = x
    for c in range(0, MLP_HIDDEN, MLP_HIDDEN_TILE):
        u = jnp.maximum(jnp.dot(h, wup_ref[:, c:c + MLP_HIDDEN_TILE], preferred_element_type=F32), 0.0)
        acc = acc + jnp.dot((u * u).astype(BF16), wdn_ref[c:c + MLP_HIDDEN_TILE, :],
                            preferred_element_type=F32)
    o_ref[...] = _rms_normed(acc, fgain_ref[...]) if final else acc


def _mlp(x2d, gain, w_up, w_down, final_gain, final):
    n = x2d.shape[0]
    row = pl.BlockSpec((ROW_TILE, D_MODEL), lambda i: (i, 0))
    return pl.pallas_call(
        functools.partial(_mlp_kernel, final=final),
        grid=(n // ROW_TILE,),
        in_specs=[row, _resident((1, D_MODEL)), _resident((D_MODEL, MLP_HIDDEN)),
                  _resident((MLP_HIDDEN, D_MODEL)), _resident((1, D_MODEL))],
        out_specs=row,
        out_shape=jax.ShapeDtypeStruct((n, D_MODEL), F32),
        compiler_params=_params(("parallel",)),
        name="mlp_final" if final else "mlp",
    )(x2d, gain.reshape(1, D_MODEL), w_up.astype(BF16), w_down.astype(BF16),
      final_gain.reshape(1, D_MODEL))


def kernel(x, mem, norm_mix, norm_mem, w_kv, w_out, norm_mlp, w_up, w_down,
           gla_w_in, gla_w_gate2, gla_b_gate, gla_out_gain,
           hgrn_w_in, hgrn_lower_bounds, hgrn_out_gain, final_norm):
    batch, seq, _ = x.shape
    assert seq % MIX_TILE == 0 and (batch * seq) % ROW_TILE == 0 and seq % ROW_TILE == 0
    x2d = x.reshape(batch * seq, D_MODEL)
    kv = _memkv(mem.reshape(batch * MEM_LEN, D_MODEL), norm_mem, w_kv.astype(BF16))
    kv = kv.reshape(DEPTH * batch, MEM_LEN, 2 * XA_DIM)

    for i in range(DEPTH):
        j = i // 2
        if i % 2 == 0:
            q, k, v, ga, gate, xq = _proj_gla(x2d, norm_mix[i], gla_w_in[j], gla_w_gate2[j],
                                              gla_b_gate[j])
            y_mix = _mixer(q, k, v, ga, gate, gla_out_gain[j],
                           batch=batch, heads=GLA_HEADS, dv=GLA_DV)
        else:
            q, k, v, ga, gate, xq = _proj_hgrn(x2d, norm_mix[i], hgrn_w_in[j],
                                               hgrn_lower_bounds, i)
            y_mix = _mixer(q, k, v, ga, gate, hgrn_out_gain[j],
                           batch=batch, heads=HGRN_HEADS, dv=HGRN_DI)
        x2d = _attn_out(x2d, y_mix, xq, kv, w_out[i], i, batch=batch)
        x2d = _mlp(x2d, norm_mlp[i], w_up[i], w_down[i], final_norm, final=(i == DEPTH - 1))
    return x2d.reshape(batch, seq, D_MODEL)
```

```python
import functools

import jax
import jax.numpy as jnp
from jax import lax
from jax.experimental import pallas as pl
from jax.experimental.pallas import tpu as pltpu

F32 = jnp.float32
BF16 = jnp.bfloat16

D_MODEL = 1024
DEPTH = 4
MEM_LEN = 256
NORM_EPS = 1e-6
GLA_HEADS = 4
GLA_DK = 128
GLA_DV = 256
GLA_GATE_RANK = 16
GLA_TAU = 16.0
HGRN_HEADS = 8
HGRN_DF = 128
HGRN_DI = 128
XA_HEADS = 4
XA_DH = 128
XA_DIM = XA_HEADS * XA_DH
MLP_HIDDEN = 4 * D_MODEL

LANES = 128
V7X_VMEM_BYTES = 64 * 1024 * 1024
VMEM_LIMIT_BYTES = V7X_VMEM_BYTES - 8 * 1024 * 1024

CHUNK = 64
ROW_TILE = 512
MIX_TILE = 2048
MLP_HIDDEN_TILE = 1024
GATE_PAD = LANES
LOG2E = 1.4426950408889634


def _params(semantics):
    return pltpu.CompilerParams(dimension_semantics=semantics, vmem_limit_bytes=VMEM_LIMIT_BYTES)


def _resident(shape):
    zeros = (0,) * len(shape)
    return pl.BlockSpec(shape, lambda *_: zeros, pipeline_mode=pl.Buffered(1))


def _rms_normed(x, gain):
    ms = jnp.mean(x * x, axis=-1, keepdims=True)
    return x * lax.rsqrt(ms + NORM_EPS) * gain


def _sigmoid(x):
    return 1.0 / (1.0 + jnp.exp(-x))


def _silu(x):
    return x * _sigmoid(x)


def _store_log2_decay(ga_ref, ghl_ref, log_decay, heads):
    g2 = log_decay * LOG2E
    ga_ref[...] = g2
    hi = g2.astype(BF16)
    lo = (g2 - hi.astype(F32)).astype(BF16)
    for h in range(heads):
        ghl_ref[:, (2 * h) * LANES:(2 * h + 1) * LANES] = hi[:, h * LANES:(h + 1) * LANES]
        ghl_ref[:, (2 * h + 1) * LANES:(2 * h + 2) * LANES] = lo[:, h * LANES:(h + 1) * LANES]


def _memkv_kernel(mem_ref, gain_ref, w_ref, o_ref):
    h = _rms_normed(mem_ref[...], gain_ref[0]).astype(BF16)
    o_ref[0] = jnp.dot(h, w_ref[0], preferred_element_type=F32).astype(BF16)


def _memkv(mem2d, norm_mem, w_kv):
    rows = mem2d.shape[0]
    return pl.pallas_call(
        _memkv_kernel,
        grid=(DEPTH,),
        in_specs=[
            _resident((rows, D_MODEL)),
            pl.BlockSpec((1, 1, D_MODEL), lambda l: (l, 0, 0)),
            pl.BlockSpec((1, D_MODEL, 2 * XA_DIM), lambda l: (l, 0, 0)),
        ],
        out_specs=pl.BlockSpec((1, rows, 2 * XA_DIM), lambda l: (l, 0, 0)),
        out_shape=jax.ShapeDtypeStruct((DEPTH, rows, 2 * XA_DIM), BF16),
        compiler_params=_params(("arbitrary",)),
        name="memkv",
    )(mem2d, norm_mem.reshape(DEPTH, 1, D_MODEL), w_kv)


_GLA_Q = (0, 512)
_GLA_K = (512, 1024)
_GLA_V = (1024, 2048)
_GLA_G = (2048, 2048 + GATE_PAD)
_GLA_R = (_GLA_G[1], _GLA_G[1] + 1024)
_GLA_XQ = (_GLA_R[1], _GLA_R[1] + XA_DIM)
_GLA_COLS = _GLA_XQ[1]


def _proj_gla_kernel(x_ref, gain_ref, w_ref, wg2_ref, bg_ref,
                     q_ref, k_ref, v_ref, ga_ref, ghl_ref, r_ref, xq_ref):
    h = _rms_normed(x_ref[...], gain_ref[...]).astype(BF16)

    def mm(cols):
        return jnp.dot(h, w_ref[:, cols[0]:cols[1]], preferred_element_type=F32)

    q_ref[...] = (mm(_GLA_Q) * (GLA_DK ** -0.5)).astype(BF16)
    k_ref[...] = mm(_GLA_K).astype(BF16)
    v_ref[...] = mm(_GLA_V).astype(BF16)
    g_lr = mm(_GLA_G).astype(BF16)
    z = jnp.dot(g_lr, wg2_ref[...], preferred_element_type=F32) + bg_ref[...]
    log_alpha = (jnp.minimum(z, 0.0) - jnp.log(1.0 + jnp.exp(-jnp.abs(z)))) * (1.0 / GLA_TAU)
    _store_log2_decay(ga_ref, ghl_ref, log_alpha, GLA_HEADS)
    r_ref[...] = _silu(mm(_GLA_R)).astype(BF16)
    xq_ref[...] = mm(_GLA_XQ).astype(BF16)


def _proj_gla(x2d, gain, w_in, w_gate2, b_gate):
    n = x2d.shape[0]
    split = [0, 512, 1024, 2048, 2048 + GLA_GATE_RANK, 3088, 3600]
    parts = [w_in[:, a:b] for a, b in zip(split[:-1], split[1:])]
    parts[3] = jnp.pad(parts[3], ((0, 0), (0, GATE_PAD - GLA_GATE_RANK)))
    w = jnp.concatenate(parts, axis=1).astype(BF16)
    wg2 = jnp.pad(w_gate2, ((0, GATE_PAD - GLA_GATE_RANK), (0, 0))).astype(BF16)
    row = lambda width: pl.BlockSpec((ROW_TILE, width), lambda i: (i, 0))
    out = lambda width, dt: jax.ShapeDtypeStruct((n, width), dt)
    return pl.pallas_call(
        _proj_gla_kernel,
        grid=(n // ROW_TILE,),
        in_specs=[row(D_MODEL), _resident((1, D_MODEL)), _resident((D_MODEL, _GLA_COLS)),
                  _resident((GATE_PAD, 512)), _resident((1, 512))],
        out_specs=[row(512), row(512), row(1024), row(512), row(1024), row(1024), row(XA_DIM)],
        out_shape=[out(512, BF16), out(512, BF16), out(1024, BF16), out(512, F32),
                   out(1024, BF16), out(1024, BF16), out(XA_DIM, BF16)],
        compiler_params=_params(("parallel",)),
        name="proj_gla",
    )(x2d, gain.reshape(1, D_MODEL), w, wg2, b_gate.reshape(1, 512))


def _proj_hgrn_kernel(x_ref, gain_ref, w_ref, lbraw_ref,
                      q_ref, k_ref, v_ref, ga_ref, ghl_ref, g_ref, xq_ref, *, layer):
    h = _rms_normed(x_ref[...], gain_ref[...]).astype(BF16)

    def mm(a, b):
        return jnp.dot(h, w_ref[:, a:b], preferred_element_type=F32)

    raw = lbraw_ref[...]
    e = jnp.exp(raw - jnp.max(raw, axis=0, keepdims=True))
    p = e / jnp.sum(e, axis=0, keepdims=True)
    lb = jnp.sum(p[1:layer + 1, :], axis=0, keepdims=True)

    q_ref[...] = (_silu(mm(0, 1024)) * (HGRN_DF ** -0.5)).astype(BF16)
    forget = lb + (1.0 - lb) * _sigmoid(mm(1024, 2048))
    k_ref[...] = (1.0 - forget).astype(BF16)
    _store_log2_decay(ga_ref, ghl_ref, jnp.log(forget), HGRN_HEADS)
    v_ref[...] = mm(2048, 3072).astype(BF16)
    g_ref[...] = _silu(mm(3072, 4096)).astype(BF16)
    xq_ref[...] = mm(4096, 4608).astype(BF16)


def _proj_hgrn(x2d, gain, w_in, lower_bounds_raw, layer):
    n = x2d.shape[0]
    row = lambda width: pl.BlockSpec((ROW_TILE, width), lambda i: (i, 0))
    out = lambda width, dt: jax.ShapeDtypeStruct((n, width), dt)
    return pl.pallas_call(
        functools.partial(_proj_hgrn_kernel, layer=layer),
        grid=(n // ROW_TILE,),
        in_specs=[row(D_MODEL), _resident((1, D_MODEL)), _resident((D_MODEL, 4608)),
                  _resident((DEPTH, 1024))],
        out_specs=[row(1024), row(1024), row(1024), row(1024), row(2048), row(1024), row(XA_DIM)],
        out_shape=[out(1024, BF16), out(1024, BF16), out(1024, BF16), out(1024, F32),
                   out(2048, BF16), out(1024, BF16), out(XA_DIM, BF16)],
        compiler_params=_params(("parallel",)),
        name="proj_hgrn",
    )(x2d, gain.reshape(1, D_MODEL), w_in.astype(BF16), lower_bounds_raw)


_NT = (((1,), (1,)), ((), ()))
_BNT = (((2,), (2,)), ((0,), (0,)))
_BNN = (((2,), (1,)), ((0,), (0,)))
_BTN = (((1,), (1,)), ((0,), (0,)))

N_CHUNKS = MIX_TILE // CHUNK
SUBLANES = 8
LEVELS = (1, 2, 4, 8, 16, 32)


def _level_rows(b, g2, q, k, n):
    if n >= SUBLANES:
        split = lambda a: a.reshape(MIX_TILE // (2 * n), 2, n, LANES)
        b4 = split(b)
        r = b4[:, 0, n - 1:n, :]
        lower = split(k)[:, 0] * jnp.exp2(r - b4[:, 0])
        upper = split(q)[:, 1] * jnp.exp2(b4[:, 1] - r)
        return jnp.stack([lower, upper], axis=1).reshape(MIX_TILE, LANES)
    tile = lambda a: a.reshape(MIX_TILE // SUBLANES, SUBLANES, LANES)
    sub = lax.broadcasted_iota(jnp.int32, (1, SUBLANES, LANES), 1)
    is_upper = (sub & n) != 0
    if n == 1:
        e = jnp.where(is_upper, tile(g2), 0.0)
    else:
        b3 = tile(b)
        r = b3[:, n - 1:n, :]
        for start in range(2 * n, SUBLANES, 2 * n):
            r = jnp.where(sub >= start, b3[:, start + n - 1:start + n, :], r)
        d = b3 - r
        e = jnp.where(is_upper, d, -d)
    x = jnp.where(is_upper, tile(q), tile(k)) * jnp.exp2(e)
    return x.reshape(MIX_TILE, LANES)


def _mixer_kernel(q_ref, k_ref, v_ref, ga_ref, ghl_ref, gate_ref, gain_ref, o_ref, s_ref, sall_ref,
                  *, dv):
    @pl.when(pl.program_id(2) == 0)
    def _():
        s_ref[...] = jnp.zeros_like(s_ref)

    shape3 = (N_CHUNKS, CHUNK, CHUNK)
    rows = lax.broadcasted_iota(jnp.int32, shape3, 1)
    cols = lax.broadcasted_iota(jnp.int32, shape3, 2)
    tri = (rows >= cols).astype(BF16)
    sep = rows ^ cols

    def chunks(a):
        return a.reshape(N_CHUNKS, CHUNK, a.shape[-1])

    bb = lax.dot_general(tri, chunks(ghl_ref[...]), _BNN, preferred_element_type=F32)
    b = (bb[:, :, :LANES] + bb[:, :, LANES:]).reshape(MIX_TILE, LANES)
    g2 = ga_ref[...]

    qb = q_ref[...]
    kb = k_ref[...]
    v = chunks(v_ref[...])
    q = qb.astype(F32)
    k = kb.astype(F32)

    attn = lax.dot_general(chunks(qb), chunks(kb), _BNT, preferred_element_type=F32)
    for n in LEVELS:
        x = chunks(_level_rows(b, g2, q, k, n).astype(BF16))
        p = lax.dot_general(x, x, _BNT, preferred_element_type=F32)
        attn = jnp.where(sep >= n, p, attn)
    attn = jnp.where(rows >= cols, attn, 0.0).astype(BF16)

    b3 = chunks(b)
    b_last = b3[:, CHUNK - 1:CHUNK, :]
    qd = (chunks(q) * jnp.exp2(b3)).astype(BF16)
    kd = (chunks(k) * jnp.exp2(b_last - b3)).astype(BF16)
    u = lax.dot_general(kd, v, _BTN, preferred_element_type=F32)
    decay = jnp.swapaxes(jnp.broadcast_to(jnp.exp2(b_last), (N_CHUNKS, LANES, LANES)), 1, 2)
    if dv > LANES:
        decay = jnp.concatenate([decay] * (dv // LANES), axis=2)

    s = s_ref[...]
    for c in range(N_CHUNKS):
        sall_ref[c] = s.astype(BF16)
        s = decay[c] * s + u[c]
    s_ref[...] = s

    o = (lax.dot_general(attn, v, _BNN, preferred_element_type=F32)
         + lax.dot_general(qd, sall_ref[...], _BNN, preferred_element_type=F32))
    y = _rms_normed(o.reshape(MIX_TILE, dv), gain_ref[...]) * gate_ref[...].astype(F32)
    o_ref[...] = y.astype(o_ref.dtype)


def _mixer(q, k, v, ga, ghl, gate, out_gain, *, batch, heads, dv):
    n = q.shape[0]
    tiles = n // batch // MIX_TILE
    kspec = pl.BlockSpec((MIX_TILE, LANES), lambda b, h, t: (b * tiles + t, h))
    hlspec = pl.BlockSpec((MIX_TILE, 2 * LANES), lambda b, h, t: (b * tiles + t, h))
    vspec = pl.BlockSpec((MIX_TILE, dv), lambda b, h, t: (b * tiles + t, h))
    return pl.pallas_call(
        functools.partial(_mixer_kernel, dv=dv),
        grid=(batch, heads, tiles),
        in_specs=[kspec, kspec, vspec, kspec, hlspec, vspec, _resident((1, dv))],
        out_specs=vspec,
        out_shape=jax.ShapeDtypeStruct((n, D_MODEL), BF16),
        scratch_shapes=[pltpu.VMEM((LANES, dv), F32), pltpu.VMEM((N_CHUNKS, LANES, dv), BF16)],
        compiler_params=_params(("parallel", "parallel", "arbitrary")),
        name=f"mixer_dv{dv}",
    )(q, k, v, ga, ghl, gate, out_gain.reshape(1, dv))


def _attn_out_kernel(x_ref, ymix_ref, xq_ref, kv_ref, wout_ref, o_ref):
    acc = x_ref[...] + jnp.dot(ymix_ref[...], wout_ref[0:D_MODEL, :], preferred_element_type=F32)
    xq = xq_ref[...]
    kv = kv_ref[0]
    heads = []
    for h in range(XA_HEADS):
        lo, hi = h * XA_DH, (h + 1) * XA_DH
        s = lax.dot_general(xq[:, lo:hi], kv[:, lo:hi], _NT, preferred_element_type=F32)
        s = s * (XA_DH ** -0.5)
        e = jnp.exp(s - jnp.max(s, axis=-1, keepdims=True)).astype(BF16)
        vh = kv[:, XA_DIM + lo:XA_DIM + hi]
        num_den = jnp.dot(e, jnp.concatenate([vh, jnp.ones_like(vh)], axis=1),
                          preferred_element_type=F32)
        heads.append((num_den[:, :XA_DH] / num_den[:, XA_DH:]).astype(BF16))
    y_mem = jnp.concatenate(heads, axis=1)
    o_ref[...] = acc + jnp.dot(y_mem, wout_ref[D_MODEL:, :], preferred_element_type=F32)


def _attn_out(x2d, y_mix, xq, kv, w_out, layer, *, batch):
    n = x2d.shape[0]
    tiles = n // batch // ROW_TILE
    row = lambda width: pl.BlockSpec((ROW_TILE, width), lambda i: (i, 0))
    return pl.pallas_call(
        _attn_out_kernel,
        grid=(n // ROW_TILE,),
        in_specs=[row(D_MODEL), row(D_MODEL), row(XA_DIM),
                  pl.BlockSpec((1, MEM_LEN, 2 * XA_DIM), lambda i: (layer * batch + i // tiles, 0, 0)),
                  _resident((D_MODEL + XA_DIM, D_MODEL))],
        out_specs=row(D_MODEL),
        out_shape=jax.ShapeDtypeStruct((n, D_MODEL), F32),
        compiler_params=_params(("parallel",)),
        name="attn_out",
    )(x2d, y_mix, xq, kv, w_out.astype(BF16))


def _mlp_kernel(x_ref, gain_ref, wup_ref, wdn_ref, fgain_ref, o_ref, *, final):
    x = x_ref[...]
    h = _rms_normed(x, gain_ref[...]).astype(BF16)
    acc = x
    for c in range(0, MLP_HIDDEN, MLP_HIDDEN_TILE):
        u = jnp.maximum(jnp.dot(h, wup_ref[:, c:c + MLP_HIDDEN_TILE], preferred_element_type=F32), 0.0)
        acc = acc + jnp.dot((u * u).astype(BF16), wdn_ref[c:c + MLP_HIDDEN_TILE, :],
                            preferred_element_type=F32)
    o_ref[...] = _rms_normed(acc, fgain_ref[...]) if final else acc


def _mlp(x2d, gain, w_up, w_down, final_gain, final):
    n = x2d.shape[0]
    row = pl.BlockSpec((ROW_TILE, D_MODEL), lambda i: (i, 0))
    return pl.pallas_call(
        functools.partial(_mlp_kernel, final=final),
        grid=(n // ROW_TILE,),
        in_specs=[row, _resident((1, D_MODEL)), _resident((D_MODEL, MLP_HIDDEN)),
                  _resident((MLP_HIDDEN, D_MODEL)), _resident((1, D_MODEL))],
        out_specs=row,
        out_shape=jax.ShapeDtypeStruct((n, D_MODEL), F32),
        compiler_params=_params(("parallel",)),
        name="mlp_final" if final else "mlp",
    )(x2d, gain.reshape(1, D_MODEL), w_up.astype(BF16), w_down.astype(BF16),
      final_gain.reshape(1, D_MODEL))


def kernel(x, mem, norm_mix, norm_mem, w_kv, w_out, norm_mlp, w_up, w_down,
           gla_w_in, gla_w_gate2, gla_b_gate, gla_out_gain,
           hgrn_w_in, hgrn_lower_bounds, hgrn_out_gain, final_norm):
    batch, seq, _ = x.shape
    assert seq % MIX_TILE == 0 and (batch * seq) % ROW_TILE == 0 and seq % ROW_TILE == 0
    x2d = x.reshape(batch * seq, D_MODEL)
    kv = _memkv(mem.reshape(batch * MEM_LEN, D_MODEL), norm_mem, w_kv.astype(BF16))
    kv = kv.reshape(DEPTH * batch, MEM_LEN, 2 * XA_DIM)

    for i in range(DEPTH):
        j = i // 2
        if i % 2 == 0:
            q, k, v, ga, ghl, gate, xq = _proj_gla(x2d, norm_mix[i], gla_w_in[j], gla_w_gate2[j],
                                                   gla_b_gate[j])
            y_mix = _mixer(q, k, v, ga, ghl, gate, gla_out_gain[j],
                           batch=batch, heads=GLA_HEADS, dv=GLA_DV)
        else:
            q, k, v, ga, ghl, gate, xq = _proj_hgrn(x2d, norm_mix[i], hgrn_w_in[j],
                                                    hgrn_lower_bounds, i)
            y_mix = _mixer(q, k, v, ga, ghl, gate, hgrn_out_gain[j],
                           batch=batch, heads=HGRN_HEADS, dv=HGRN_DI)
        x2d = _attn_out(x2d, y_mix, xq, kv, w_out[i], i, batch=batch)
        x2d = _mlp(x2d, norm_mlp[i], w_up[i], w_down[i], final_norm, final=(i == DEPTH - 1))
    return x2d.reshape(batch, seq, D_MODEL)
```

```python
import functools

import jax
import jax.numpy as jnp
from jax import lax
from jax.experimental import pallas as pl
from jax.experimental.pallas import tpu as pltpu

F32 = jnp.float32
BF16 = jnp.bfloat16

D_MODEL = 1024
DEPTH = 4
MEM_LEN = 256
NORM_EPS = 1e-6
GLA_HEADS = 4
GLA_DK = 128
GLA_DV = 256
GLA_GATE_RANK = 16
GLA_TAU = 16.0
HGRN_HEADS = 8
HGRN_DF = 128
HGRN_DI = 128
XA_HEADS = 4
XA_DH = 128
XA_DIM = XA_HEADS * XA_DH
MLP_HIDDEN = 4 * D_MODEL

LANES = 128
SUBLANES = 8
V7X_VMEM_BYTES = 64 * 1024 * 1024
VMEM_LIMIT_BYTES = V7X_VMEM_BYTES - 8 * 1024 * 1024

CHUNK = 64
ROW_TILE = 512
MIX_TILE = 512
N_CHUNKS = MIX_TILE // CHUNK
LEVELS = (1, 2, 4, 8, 16, 32)
MLP_HIDDEN_TILE = 1024
GATE_PAD = LANES
LOG2E = 1.4426950408889634

_NT = (((1,), (1,)), ((), ()))
_BNT = (((2,), (2,)), ((0,), (0,)))
_BNN = (((2,), (1,)), ((0,), (0,)))
_BTN = (((1,), (1,)), ((0,), (0,)))


def _params(semantics, flags=None):
    return pltpu.CompilerParams(dimension_semantics=semantics, vmem_limit_bytes=VMEM_LIMIT_BYTES,
                                flags=flags)


def _resident(shape):
    zeros = (0,) * len(shape)
    return pl.BlockSpec(shape, lambda *_: zeros, pipeline_mode=pl.Buffered(1))


def _rms_normed(x, gain):
    ms = jnp.mean(x * x, axis=-1, keepdims=True)
    return x * lax.rsqrt(ms + NORM_EPS) * gain


def _sigmoid(x):
    return 1.0 / (1.0 + jnp.exp(-x))


def _silu(x):
    return x * _sigmoid(x)


def _memkv_kernel(mem_ref, gain_ref, w_ref, o_ref):
    h = _rms_normed(mem_ref[...], gain_ref[0]).astype(BF16)
    o_ref[0] = jnp.dot(h, w_ref[0], preferred_element_type=F32).astype(BF16)


def _memkv(mem2d, norm_mem, w_kv):
    rows = mem2d.shape[0]
    return pl.pallas_call(
        _memkv_kernel,
        grid=(DEPTH,),
        in_specs=[
            _resident((rows, D_MODEL)),
            pl.BlockSpec((1, 1, D_MODEL), lambda l: (l, 0, 0)),
            pl.BlockSpec((1, D_MODEL, 2 * XA_DIM), lambda l: (l, 0, 0)),
        ],
        out_specs=pl.BlockSpec((1, rows, 2 * XA_DIM), lambda l: (l, 0, 0)),
        out_shape=jax.ShapeDtypeStruct((DEPTH, rows, 2 * XA_DIM), BF16),
        compiler_params=_params(("arbitrary",)),
        name="memkv",
    )(mem2d, norm_mem.reshape(DEPTH, 1, D_MODEL), w_kv)


def _chunks(a):
    return a.reshape(N_CHUNKS, CHUNK, a.shape[-1])


def _level_rows(b, g2, q, k, n):
    if n >= SUBLANES:
        split = lambda a: a.reshape(MIX_TILE // (2 * n), 2, n, LANES)
        b4 = split(b)
        r = b4[:, 0, n - 1:n, :]
        lower = split(k)[:, 0] * jnp.exp2(r - b4[:, 0])
        upper = split(q)[:, 1] * jnp.exp2(b4[:, 1] - r)
        return jnp.stack([lower, upper], axis=1).reshape(MIX_TILE, LANES)
    tile = lambda a: a.reshape(MIX_TILE // SUBLANES, SUBLANES, LANES)
    sub = lax.broadcasted_iota(jnp.int32, (1, SUBLANES, LANES), 1)
    is_upper = (sub & n) != 0
    if n == 1:
        e = jnp.where(is_upper, tile(g2), 0.0)
    else:
        b3 = tile(b)
        r = b3[:, n - 1:n, :]
        for start in range(2 * n, SUBLANES, 2 * n):
            r = jnp.where(sub >= start, b3[:, start + n - 1:start + n, :], r)
        d = b3 - r
        e = jnp.where(is_upper, d, -d)
    x = jnp.where(is_upper, tile(q), tile(k)) * jnp.exp2(e)
    return x.reshape(MIX_TILE, LANES)


def _recurrence(q, k, v, g2, s_ref, sall_ref, dv, independent_matmul):
    shape3 = (N_CHUNKS, CHUNK, CHUNK)
    rows = lax.broadcasted_iota(jnp.int32, shape3, 1)
    cols = lax.broadcasted_iota(jnp.int32, shape3, 2)
    tri = (rows >= cols).astype(BF16)
    sep = rows ^ cols

    g_hi = g2.astype(BF16)
    g_lo = (g2 - g_hi.astype(F32)).astype(BF16)
    bb = lax.dot_general(tri, _chunks(jnp.concatenate([g_hi, g_lo], axis=1)), _BNN,
                         preferred_element_type=F32)
    b = (bb[:, :, :LANES] + bb[:, :, LANES:]).reshape(MIX_TILE, LANES)
    independent = independent_matmul()

    v = _chunks(v)
    attn = lax.dot_general(_chunks(q.astype(BF16)), _chunks(k.astype(BF16)), _BNT,
                           preferred_element_type=F32)
    for n in LEVELS:
        x = _chunks(_level_rows(b, g2, q, k, n).astype(BF16))
        p = lax.dot_general(x, x, _BNT, preferred_element_type=F32)
        attn = jnp.where(sep >= n, p, attn)
    attn = jnp.where(rows >= cols, attn, 0.0).astype(BF16)

    b3 = _chunks(b)
    b_last = b3[:, CHUNK - 1:CHUNK, :]
    qd = (_chunks(q) * jnp.exp2(b3)).astype(BF16)
    kd = (_chunks(k) * jnp.exp2(b_last - b3)).astype(BF16)
    u = lax.dot_general(kd, v, _BTN, preferred_element_type=F32)
    decay = jnp.swapaxes(jnp.broadcast_to(jnp.exp2(b_last), (N_CHUNKS, LANES, LANES)), 1, 2)
    if dv > LANES:
        decay = jnp.concatenate([decay] * (dv // LANES), axis=2)

    s = s_ref[...]
    for c in range(N_CHUNKS):
        sall_ref[c] = s.astype(BF16)
        s = decay[c] * s + u[c]
    s_ref[...] = s

    o = (lax.dot_general(attn, v, _BNN, preferred_element_type=F32)
         + lax.dot_general(qd, sall_ref[...], _BNN, preferred_element_type=F32))
    return o.reshape(MIX_TILE, dv), independent


MIX_FLAGS = None
_GLA_HEAD_COLS = 2 * GLA_DK + 2 * GLA_DV
_HGRN_HEAD_COLS = 4 * LANES


def _mix_gla_kernel(x_ref, gain_ref, w_ref, wglr_ref, wg2_ref, bg_ref, wxq_ref, ogain_ref,
                    y_ref, xq_ref, s_ref, sall_ref):
    @pl.when(pl.program_id(1) == 0)
    def _():
        s_ref[...] = jnp.zeros_like(s_ref)

    h = _rms_normed(x_ref[...], gain_ref[...]).astype(BF16)

    def project(group):
        w = w_ref[group] if group < GLA_HEADS else wxq_ref[...]
        return jnp.dot(h, w, preferred_element_type=F32)

    g_lr = jnp.dot(h, wglr_ref[...], preferred_element_type=F32).astype(BF16)
    cols = project(0)
    for head in range(GLA_HEADS):
        lanes = slice(head * LANES, (head + 1) * LANES)
        z = jnp.dot(g_lr, wg2_ref[:, lanes], preferred_element_type=F32) + bg_ref[:, lanes]
        g2 = (jnp.minimum(z, 0.0) - jnp.log(1.0 + jnp.exp(-jnp.abs(z)))) * (LOG2E / GLA_TAU)
        q = cols[:, 0:GLA_DK] * (GLA_DK ** -0.5)
        k = cols[:, GLA_DK:2 * GLA_DK]
        v = cols[:, 2 * GLA_DK:2 * GLA_DK + GLA_DV].astype(BF16)
        gate = _silu(cols[:, 2 * GLA_DK + GLA_DV:])
        o, cols = _recurrence(q, k, v, g2, s_ref.at[head], sall_ref.at[head], GLA_DV,
                              functools.partial(project, head + 1))
        y = _rms_normed(o, ogain_ref[...]) * gate
        y_ref[:, head * GLA_DV:(head + 1) * GLA_DV] = y.astype(y_ref.dtype)
    xq_ref[...] = cols.astype(BF16)


def _mix_hgrn_kernel(x_ref, gain_ref, w_ref, lbraw_ref, wxq_ref, ogain_ref,
                     y_ref, xq_ref, s_ref, sall_ref, *, layer):
    @pl.when(pl.program_id(1) == 0)
    def _():
        s_ref[...] = jnp.zeros_like(s_ref)

    raw = lbraw_ref[...]
    e = jnp.exp(raw - jnp.max(raw, axis=0, keepdims=True))
    p = e / jnp.sum(e, axis=0, keepdims=True)
    lb = jnp.sum(p[1:layer + 1, :], axis=0, keepdims=True)

    h = _rms_normed(x_ref[...], gain_ref[...]).astype(BF16)

    def project(group):
        w = w_ref[group] if group < HGRN_HEADS else wxq_ref[...]
        return jnp.dot(h, w, preferred_element_type=F32)

    cols = project(0)
    for head in range(HGRN_HEADS):
        lanes = slice(head * LANES, (head + 1) * LANES)
        q = _silu(cols[:, 0:LANES]) * (HGRN_DF ** -0.5)
        forget = lb[:, lanes] + (1.0 - lb[:, lanes]) * _sigmoid(cols[:, LANES:2 * LANES])
        k = 1.0 - forget
        g2 = jnp.log2(forget)
        v = cols[:, 2 * LANES:3 * LANES].astype(BF16)
        gate = _silu(cols[:, 3 * LANES:])
        o, cols = _recurrence(q, k, v, g2, s_ref.at[head], sall_ref.at[head], HGRN_DI,
                              functools.partial(project, head + 1))
        y = _rms_normed(o, ogain_ref[...]) * gate
        y_ref[:, lanes] = y.astype(y_ref.dtype)
    xq_ref[...] = cols.astype(BF16)


def _mix_call(body, heads, dv, head_cols, x2d, batch, operands, specs, name):
    n = x2d.shape[0]
    tiles = n // batch // MIX_TILE
    row = lambda width: pl.BlockSpec((MIX_TILE, width), lambda b, t: (b * tiles + t, 0))
    return pl.pallas_call(
        body,
        grid=(batch, tiles),
        in_specs=[row(D_MODEL), _resident((1, D_MODEL)), _resident((heads, D_MODEL, head_cols))] + specs,
        out_specs=[row(D_MODEL), row(XA_DIM)],
        out_shape=[jax.ShapeDtypeStruct((n, D_MODEL), BF16), jax.ShapeDtypeStruct((n, XA_DIM), BF16)],
        scratch_shapes=[pltpu.VMEM((heads, LANES, dv), F32),
                        pltpu.VMEM((heads, N_CHUNKS, LANES, dv), BF16)],
        compiler_params=_params(("parallel", "arbitrary"), MIX_FLAGS),
        name=name,
    )(x2d, *operands)


def _mix_gla(x2d, gain, w_in, w_gate2, b_gate, out_gain, *, batch):
    dq, dvs = GLA_HEADS * GLA_DK, GLA_HEADS * GLA_DV
    heads = lambda a, width: a.reshape(D_MODEL, GLA_HEADS, width)
    r0 = 2 * dq + dvs + GLA_GATE_RANK
    w_heads = jnp.concatenate(
        [heads(w_in[:, 0:dq], GLA_DK), heads(w_in[:, dq:2 * dq], GLA_DK),
         heads(w_in[:, 2 * dq:2 * dq + dvs], GLA_DV), heads(w_in[:, r0:r0 + dvs], GLA_DV)],
        axis=2).transpose(1, 0, 2).astype(BF16)
    w_glr = jnp.pad(w_in[:, 2 * dq + dvs:r0], ((0, 0), (0, GATE_PAD - GLA_GATE_RANK))).astype(BF16)
    w_g2 = jnp.pad(w_gate2, ((0, GATE_PAD - GLA_GATE_RANK), (0, 0))).astype(BF16)
    w_xq = w_in[:, r0 + dvs:].astype(BF16)
    operands = (gain.reshape(1, D_MODEL), w_heads, w_glr, w_g2, b_gate.reshape(1, dq), w_xq,
                out_gain.reshape(1, GLA_DV))
    specs = [_resident((D_MODEL, GATE_PAD)), _resident((GATE_PAD, dq)), _resident((1, dq)),
             _resident((D_MODEL, XA_DIM)), _resident((1, GLA_DV))]
    return _mix_call(_mix_gla_kernel, GLA_HEADS, GLA_DV, _GLA_HEAD_COLS, x2d, batch, operands,
                     specs, "mix_gla")


def _mix_hgrn(x2d, gain, w_in, lower_bounds_raw, out_gain, layer, *, batch):
    width = HGRN_HEADS * LANES
    w_heads = (w_in[:, :4 * width].reshape(D_MODEL, 4, HGRN_HEADS, LANES)
               .transpose(2, 0, 1, 3).reshape(HGRN_HEADS, D_MODEL, _HGRN_HEAD_COLS).astype(BF16))
    w_xq = w_in[:, 4 * width:].astype(BF16)
    operands = (gain.reshape(1, D_MODEL), w_heads, lower_bounds_raw, w_xq,
                out_gain.reshape(1, HGRN_DI))
    specs = [_resident((DEPTH, width)), _resident((D_MODEL, XA_DIM)), _resident((1, HGRN_DI))]
    return _mix_call(functools.partial(_mix_hgrn_kernel, layer=layer), HGRN_HEADS, HGRN_DI,
                     _HGRN_HEAD_COLS, x2d, batch, operands, specs, "mix_hgrn")


def _attn_out_kernel(x_ref, ymix_ref, xq_ref, kv_ref, wout_ref, o_ref):
    acc = x_ref[...] + jnp.dot(ymix_ref[...], wout_ref[0:D_MODEL, :], preferred_element_type=F32)
    xq = xq_ref[...]
    kv = kv_ref[0]
    heads = []
    for h in range(XA_HEADS):
        lo, hi = h * XA_DH, (h + 1) * XA_DH
        s = lax.dot_general(xq[:, lo:hi], kv[:, lo:hi], _NT, preferred_element_type=F32)
        s = s * (XA_DH ** -0.5)
        e = jnp.exp(s - jnp.max(s, axis=-1, keepdims=True)).astype(BF16)
        vh = kv[:, XA_DIM + lo:XA_DIM + hi]
        num_den = jnp.dot(e, jnp.concatenate([vh, jnp.ones_like(vh)], axis=1),
                          preferred_element_type=F32)
        heads.append((num_den[:, :XA_DH] / num_den[:, XA_DH:]).astype(BF16))
    y_mem = jnp.concatenate(heads, axis=1)
    o_ref[...] = acc + jnp.dot(y_mem, wout_ref[D_MODEL:, :], preferred_element_type=F32)


def _attn_out(x2d, y_mix, xq, kv, w_out, layer, *, batch):
    n = x2d.shape[0]
    tiles = n // batch // ROW_TILE
    row = lambda width: pl.BlockSpec((ROW_TILE, width), lambda i: (i, 0))
    return pl.pallas_call(
        _attn_out_kernel,
        grid=(n // ROW_TILE,),
        in_specs=[row(D_MODEL), row(D_MODEL), row(XA_DIM),
                  pl.BlockSpec((1, MEM_LEN, 2 * XA_DIM), lambda i: (layer * batch + i // tiles, 0, 0)),
                  _resident((D_MODEL + XA_DIM, D_MODEL))],
        out_specs=row(D_MODEL),
        out_shape=jax.ShapeDtypeStruct((n, D_MODEL), F32),
        compiler_params=_params(("parallel",)),
        name="attn_out",
    )(x2d, y_mix, xq, kv, w_out.astype(BF16))


def _mlp_kernel(x_ref, gain_ref, wup_ref, wdn_ref, fgain_ref, o_ref, *, final):
    x = x_ref[...]
    h = _rms_normed(x, gain_ref[...]).astype(BF16)
    acc = x
    for c in range(0, MLP_HIDDEN, MLP_HIDDEN_TILE):
        u = jnp.maximum(jnp.dot(h, wup_ref[:, c:c + MLP_HIDDEN_TILE], preferred_element_type=F32), 0.0)
        acc = acc + jnp.dot((u * u).astype(BF16), wdn_ref[c:c + MLP_HIDDEN_TILE, :],
                            preferred_element_type=F32)
    o_ref[...] = _rms_normed(acc, fgain_ref[...]) if final else acc


def _mlp(x2d, gain, w_up, w_down, final_gain, final):
    n = x2d.shape[0]
    row = pl.BlockSpec((ROW_TILE, D_MODEL), lambda i: (i, 0))
    return pl.pallas_call(
        functools.partial(_mlp_kernel, final=final),
        grid=(n // ROW_TILE,),
        in_specs=[row, _resident((1, D_MODEL)), _resident((D_MODEL, MLP_HIDDEN)),
                  _resident((MLP_HIDDEN, D_MODEL)), _resident((1, D_MODEL))],
        out_specs=row,
        out_shape=jax.ShapeDtypeStruct((n, D_MODEL), F32),
        compiler_params=_params(("parallel",)),
        name="mlp_final" if final else "mlp",
    )(x2d, gain.reshape(1, D_MODEL), w_up.astype(BF16), w_down.astype(BF16),
      final_gain.reshape(1, D_MODEL))


def kernel(x, mem, norm_mix, norm_mem, w_kv, w_out, norm_mlp, w_up, w_down,
           gla_w_in, gla_w_gate2, gla_b_gate, gla_out_gain,
           hgrn_w_in, hgrn_lower_bounds, hgrn_out_gain, final_norm):
    batch, seq, _ = x.shape
    assert seq % MIX_TILE == 0 and seq % ROW_TILE == 0
    x2d = x.reshape(batch * seq, D_MODEL)
    kv = _memkv(mem.reshape(batch * MEM_LEN, D_MODEL), norm_mem, w_kv.astype(BF16))
    kv = kv.reshape(DEPTH * batch, MEM_LEN, 2 * XA_DIM)

    for i in range(DEPTH):
        j = i // 2
        if i % 2 == 0:
            y_mix, xq = _mix_gla(x2d, norm_mix[i], gla_w_in[j], gla_w_gate2[j], gla_b_gate[j],
                                 gla_out_gain[j], batch=batch)
        else:
            y_mix, xq = _mix_hgrn(x2d, norm_mix[i], hgrn_w_in[j], hgrn_lower_bounds,
                                  hgrn_out_gain[j], i, batch=batch)
        x2d = _attn_out(x2d, y_mix, xq, kv, w_out[i], i, batch=batch)
        x2d = _mlp(x2d, norm_mlp[i], w_up[i], w_down[i], final_norm, final=(i == DEPTH - 1))
    return x2d.reshape(batch, seq, D_MODEL)
```

```python
import functools

import jax
import jax.numpy as jnp
from jax import lax
from jax.experimental import pallas as pl
from jax.experimental.pallas import tpu as pltpu

F32 = jnp.float32
BF16 = jnp.bfloat16

D_MODEL = 1024
DEPTH = 4
MEM_LEN = 256
NORM_EPS = 1e-6
GLA_HEADS = 4
GLA_DK = 128
GLA_DV = 256
GLA_GATE_RANK = 16
GLA_TAU = 16.0
HGRN_HEADS = 8
HGRN_DF = 128
HGRN_DI = 128
XA_HEADS = 4
XA_DH = 128
XA_DIM = XA_HEADS * XA_DH
MLP_HIDDEN = 4 * D_MODEL

LANES = 128
SUBLANES = 8
V7X_VMEM_BYTES = 64 * 1024 * 1024
VMEM_LIMIT_BYTES = V7X_VMEM_BYTES - 8 * 1024 * 1024

CHUNK = 64
ROW_TILE = 512
ATTN_TILE = 1024
MIX_TILE = 4096
N_CHUNKS = MIX_TILE // CHUNK
LEVELS = (1, 2, 4, 8, 16, 32)
MLP_HIDDEN_TILE = 1024
GATE_PAD = LANES
LOG2E = 1.4426950408889634

_NT = (((1,), (1,)), ((), ()))
_BNT = (((2,), (2,)), ((0,), (0,)))
_BNN = (((2,), (1,)), ((0,), (0,)))
_BTN = (((1,), (1,)), ((0,), (0,)))


def _params(semantics):
    return pltpu.CompilerParams(dimension_semantics=semantics, vmem_limit_bytes=VMEM_LIMIT_BYTES)


def _resident(shape):
    zeros = (0,) * len(shape)
    return pl.BlockSpec(shape, lambda *_: zeros, pipeline_mode=pl.Buffered(1))


def _rms_normed(x, gain):
    ms = jnp.mean(x * x, axis=-1, keepdims=True)
    return x * lax.rsqrt(ms + NORM_EPS) * gain


def _sigmoid(x):
    return 0.5 * jnp.tanh(0.5 * x) + 0.5


def _silu(x):
    half = 0.5 * x
    return half * jnp.tanh(half) + half


def _store_log2_decay(ga_ref, ghl_ref, log_decay, heads):
    g2 = log_decay * LOG2E
    ga_ref[...] = g2
    hi = g2.astype(BF16)
    lo = (g2 - hi.astype(F32)).astype(BF16)
    for h in range(heads):
        ghl_ref[:, (2 * h) * LANES:(2 * h + 1) * LANES] = hi[:, h * LANES:(h + 1) * LANES]
        ghl_ref[:, (2 * h + 1) * LANES:(2 * h + 2) * LANES] = lo[:, h * LANES:(h + 1) * LANES]


def _memkv_kernel(mem_ref, gain_ref, w_ref, o_ref):
    h = _rms_normed(mem_ref[...], gain_ref[0]).astype(BF16)
    o_ref[0] = jnp.dot(h, w_ref[0], preferred_element_type=F32).astype(BF16)


def _memkv(mem2d, norm_mem, w_kv):
    rows = mem2d.shape[0]
    return pl.pallas_call(
        _memkv_kernel,
        grid=(DEPTH,),
        in_specs=[
            _resident((rows, D_MODEL)),
            pl.BlockSpec((1, 1, D_MODEL), lambda l: (l, 0, 0)),
            pl.BlockSpec((1, D_MODEL, 2 * XA_DIM), lambda l: (l, 0, 0)),
        ],
        out_specs=pl.BlockSpec((1, rows, 2 * XA_DIM), lambda l: (l, 0, 0)),
        out_shape=jax.ShapeDtypeStruct((DEPTH, rows, 2 * XA_DIM), BF16),
        compiler_params=_params(("arbitrary",)),
        name="memkv",
    )(mem2d, norm_mem.reshape(DEPTH, 1, D_MODEL), w_kv)


_GLA_Q = (0, 512)
_GLA_K = (512, 1024)
_GLA_V = (1024, 2048)
_GLA_G = (2048, 2048 + GATE_PAD)
_GLA_R = (_GLA_G[1], _GLA_G[1] + 1024)
_GLA_XQ = (_GLA_R[1], _GLA_R[1] + XA_DIM)
_GLA_COLS = _GLA_XQ[1]


def _proj_gla_kernel(x_ref, gain_ref, w_ref, wg2_ref, bg_ref,
                     q_ref, k_ref, v_ref, ga_ref, ghl_ref, r_ref, xq_ref):
    h = _rms_normed(x_ref[...], gain_ref[...]).astype(BF16)

    def mm(cols):
        return jnp.dot(h, w_ref[:, cols[0]:cols[1]], preferred_element_type=F32)

    q_ref[...] = (mm(_GLA_Q) * (GLA_DK ** -0.5)).astype(BF16)
    k_ref[...] = mm(_GLA_K).astype(BF16)
    v_ref[...] = mm(_GLA_V).astype(BF16)
    g_lr = mm(_GLA_G).astype(BF16)
    z = jnp.dot(g_lr, wg2_ref[...], preferred_element_type=F32) + bg_ref[...]
    log_alpha = (jnp.minimum(z, 0.0) - jnp.log(1.0 + jnp.exp(-jnp.abs(z)))) * (1.0 / GLA_TAU)
    _store_log2_decay(ga_ref, ghl_ref, log_alpha, GLA_HEADS)
    r_ref[...] = _silu(mm(_GLA_R)).astype(BF16)
    xq_ref[...] = mm(_GLA_XQ).astype(BF16)


def _proj_gla(x2d, gain, w_in, w_gate2, b_gate):
    n = x2d.shape[0]
    split = [0, 512, 1024, 2048, 2048 + GLA_GATE_RANK, 3088, 3600]
    parts = [w_in[:, a:b] for a, b in zip(split[:-1], split[1:])]
    parts[3] = jnp.pad(parts[3], ((0, 0), (0, GATE_PAD - GLA_GATE_RANK)))
    w = jnp.concatenate(parts, axis=1).astype(BF16)
    wg2 = jnp.pad(w_gate2, ((0, GATE_PAD - GLA_GATE_RANK), (0, 0))).astype(BF16)
    row = lambda width: pl.BlockSpec((ROW_TILE, width), lambda i: (i, 0))
    out = lambda width, dt: jax.ShapeDtypeStruct((n, width), dt)
    return pl.pallas_call(
        _proj_gla_kernel,
        grid=(n // ROW_TILE,),
        in_specs=[row(D_MODEL), _resident((1, D_MODEL)), _resident((D_MODEL, _GLA_COLS)),
                  _resident((GATE_PAD, 512)), _resident((1, 512))],
        out_specs=[row(512), row(512), row(1024), row(512), row(1024), row(1024), row(XA_DIM)],
        out_shape=[out(512, BF16), out(512, BF16), out(1024, BF16), out(512, F32),
                   out(1024, BF16), out(1024, BF16), out(XA_DIM, BF16)],
        compiler_params=_params(("parallel",)),
        name="proj_gla",
    )(x2d, gain.reshape(1, D_MODEL), w, wg2, b_gate.reshape(1, 512))


def _proj_hgrn_kernel(x_ref, gain_ref, w_ref, lbraw_ref,
                      q_ref, k_ref, v_ref, ga_ref, ghl_ref, g_ref, xq_ref, *, layer):
    h = _rms_normed(x_ref[...], gain_ref[...]).astype(BF16)

    def mm(a, b):
        return jnp.dot(h, w_ref[:, a:b], preferred_element_type=F32)

    raw = lbraw_ref[...]
    e = jnp.exp(raw - jnp.max(raw, axis=0, keepdims=True))
    p = e / jnp.sum(e, axis=0, keepdims=True)
    lb = jnp.sum(p[1:layer + 1, :], axis=0, keepdims=True)

    q_ref[...] = (_silu(mm(0, 1024)) * (HGRN_DF ** -0.5)).astype(BF16)
    forget = lb + (1.0 - lb) * _sigmoid(mm(1024, 2048))
    k_ref[...] = (1.0 - forget).astype(BF16)
    _store_log2_decay(ga_ref, ghl_ref, jnp.log(forget), HGRN_HEADS)
    v_ref[...] = mm(2048, 3072).astype(BF16)
    g_ref[...] = _silu(mm(3072, 4096)).astype(BF16)
    xq_ref[...] = mm(4096, 4608).astype(BF16)


def _proj_hgrn(x2d, gain, w_in, lower_bounds_raw, layer):
    n = x2d.shape[0]
    row = lambda width: pl.BlockSpec((ROW_TILE, width), lambda i: (i, 0))
    out = lambda width, dt: jax.ShapeDtypeStruct((n, width), dt)
    return pl.pallas_call(
        functools.partial(_proj_hgrn_kernel, layer=layer),
        grid=(n // ROW_TILE,),
        in_specs=[row(D_MODEL), _resident((1, D_MODEL)), _resident((D_MODEL, 4608)),
                  _resident((DEPTH, 1024))],
        out_specs=[row(1024), row(1024), row(1024), row(1024), row(2048), row(1024), row(XA_DIM)],
        out_shape=[out(1024, BF16), out(1024, BF16), out(1024, BF16), out(1024, F32),
                   out(2048, BF16), out(1024, BF16), out(XA_DIM, BF16)],
        compiler_params=_params(("parallel",)),
        name="proj_hgrn",
    )(x2d, gain.reshape(1, D_MODEL), w_in.astype(BF16), lower_bounds_raw)


def _level_rows(b, g2, q, k, n):
    if n >= SUBLANES:
        split = lambda a: a.reshape(MIX_TILE // (2 * n), 2, n, LANES)
        b4 = split(b)
        r = b4[:, 0, n - 1:n, :]
        lower = split(k)[:, 0] * jnp.exp2(r - b4[:, 0])
        upper = split(q)[:, 1] * jnp.exp2(b4[:, 1] - r)
        return jnp.stack([lower, upper], axis=1).reshape(MIX_TILE, LANES)
    tile = lambda a: a.reshape(MIX_TILE // SUBLANES, SUBLANES, LANES)
    sub = lax.broadcasted_iota(jnp.int32, (1, SUBLANES, LANES), 1)
    is_upper = (sub & n) != 0
    if n == 1:
        e = jnp.where(is_upper, tile(g2), 0.0)
    else:
        b3 = tile(b)
        r = b3[:, n - 1:n, :]
        for start in range(2 * n, SUBLANES, 2 * n):
            r = jnp.where(sub >= start, b3[:, start + n - 1:start + n, :], r)
        d = b3 - r
        e = jnp.where(is_upper, d, -d)
    x = jnp.where(is_upper, tile(q), tile(k)) * jnp.exp2(e)
    return x.reshape(MIX_TILE, LANES)


def _mixer_kernel(q_ref, k_ref, v_ref, ga_ref, ghl_ref, gate_ref, gain_ref, o_ref, s_ref, sall_ref,
                  *, dv):
    @pl.when(pl.program_id(2) == 0)
    def _():
        s_ref[...] = jnp.zeros_like(s_ref)

    shape3 = (N_CHUNKS, CHUNK, CHUNK)
    rows = lax.broadcasted_iota(jnp.int32, shape3, 1)
    cols = lax.broadcasted_iota(jnp.int32, shape3, 2)
    tri = (rows >= cols).astype(BF16)
    sep = rows ^ cols

    def chunks(a):
        return a.reshape(N_CHUNKS, CHUNK, a.shape[-1])

    bb = lax.dot_general(tri, chunks(ghl_ref[...]), _BNN, preferred_element_type=F32)
    b = (bb[:, :, :LANES] + bb[:, :, LANES:]).reshape(MIX_TILE, LANES)
    g2 = ga_ref[...]

    qb = q_ref[...]
    kb = k_ref[...]
    v = chunks(v_ref[...])
    q = qb.astype(F32)
    k = kb.astype(F32)

    attn = lax.dot_general(chunks(qb), chunks(kb), _BNT, preferred_element_type=F32)
    for n in LEVELS:
        x = chunks(_level_rows(b, g2, q, k, n).astype(BF16))
        p = lax.dot_general(x, x, _BNT, preferred_element_type=F32)
        attn = jnp.where(sep >= n, p, attn)
    attn = jnp.where(rows >= cols, attn, 0.0).astype(BF16)

    b3 = chunks(b)
    b_last = b3[:, CHUNK - 1:CHUNK, :]
    qd = (chunks(q) * jnp.exp2(b3)).astype(BF16)
    kd = (chunks(k) * jnp.exp2(b_last - b3)).astype(BF16)
    u = lax.dot_general(kd, v, _BTN, preferred_element_type=F32)
    decay = jnp.swapaxes(jnp.broadcast_to(jnp.exp2(b_last), (N_CHUNKS, LANES, LANES)), 1, 2)
    if dv > LANES:
        decay = jnp.concatenate([decay] * (dv // LANES), axis=2)

    s = s_ref[...]
    for c in range(N_CHUNKS):
        sall_ref[c] = s.astype(BF16)
        s = decay[c] * s + u[c]
    s_ref[...] = s

    o = (lax.dot_general(attn, v, _BNN, preferred_element_type=F32)
         + lax.dot_general(qd, sall_ref[...], _BNN, preferred_element_type=F32))
    y = _rms_normed(o.reshape(MIX_TILE, dv), gain_ref[...]) * gate_ref[...].astype(F32)
    o_ref[...] = y.astype(o_ref.dtype)


def _mixer(q, k, v, ga, ghl, gate, out_gain, *, batch, heads, dv):
    n = q.shape[0]
    tiles = n // batch // MIX_TILE
    kspec = pl.BlockSpec((MIX_TILE, LANES), lambda b, h, t: (b * tiles + t, h))
    hlspec = pl.BlockSpec((MIX_TILE, 2 * LANES), lambda b, h, t: (b * tiles + t, h))
    vspec = pl.BlockSpec((MIX_TILE, dv), lambda b, h, t: (b * tiles + t, h))
    return pl.pallas_call(
        functools.partial(_mixer_kernel, dv=dv),
        grid=(batch, heads, tiles),
        in_specs=[kspec, kspec, vspec, kspec, hlspec, vspec, _resident((1, dv))],
        out_specs=vspec,
        out_shape=jax.ShapeDtypeStruct((n, D_MODEL), BF16),
        scratch_shapes=[pltpu.VMEM((LANES, dv), F32), pltpu.VMEM((N_CHUNKS, LANES, dv), BF16)],
        compiler_params=_params(("parallel", "parallel", "arbitrary")),
        name=f"mixer_dv{dv}",
    )(q, k, v, ga, ghl, gate, out_gain.reshape(1, dv))


def _attn_mlp_kernel(x_ref, ymix_ref, xq_ref, kv_ref, wout_ref, gain_ref, wup_ref, wdn_ref,
                     fgain_ref, o_ref, *, final):
    acc = x_ref[...] + jnp.dot(ymix_ref[...], wout_ref[0:D_MODEL, :], preferred_element_type=F32)
    xq = xq_ref[...]
    kv = kv_ref[0]
    heads = []
    for h in range(XA_HEADS):
        lo, hi = h * XA_DH, (h + 1) * XA_DH
        s = lax.dot_general(xq[:, lo:hi], kv[:, lo:hi], _NT, preferred_element_type=F32)
        s = s * (XA_DH ** -0.5)
        e = jnp.exp(s - jnp.max(s, axis=-1, keepdims=True)).astype(BF16)
        vh = kv[:, XA_DIM + lo:XA_DIM + hi]
        num_den = jnp.dot(e, jnp.concatenate([vh, jnp.ones_like(vh)], axis=1),
                          preferred_element_type=F32)
        heads.append((num_den[:, :XA_DH] / num_den[:, XA_DH:]).astype(BF16))
    y_mem = jnp.concatenate(heads, axis=1)
    x = acc + jnp.dot(y_mem, wout_ref[D_MODEL:, :], preferred_element_type=F32)

    h = _rms_normed(x, gain_ref[...]).astype(BF16)
    mlp = None
    for c in range(0, MLP_HIDDEN, MLP_HIDDEN_TILE):
        u = jnp.maximum(jnp.dot(h, wup_ref[:, c:c + MLP_HIDDEN_TILE], preferred_element_type=F32), 0.0)
        d = jnp.dot((u * u).astype(BF16), wdn_ref[c:c + MLP_HIDDEN_TILE, :],
                    preferred_element_type=F32)
        mlp = d if mlp is None else mlp + d
    out = x + mlp
    o_ref[...] = _rms_normed(out, fgain_ref[...]) if final else out


def _attn_mlp(x2d, y_mix, xq, kv, w_out, gain, w_up, w_down, final_gain, layer, *, batch, final):
    n = x2d.shape[0]
    tiles = n // batch // ATTN_TILE
    row = lambda width: pl.BlockSpec((ATTN_TILE, width), lambda i: (i, 0))
    return pl.pallas_call(
        functools.partial(_attn_mlp_kernel, final=final),
        grid=(n // ATTN_TILE,),
        in_specs=[row(D_MODEL), row(D_MODEL), row(XA_DIM),
                  pl.BlockSpec((1, MEM_LEN, 2 * XA_DIM), lambda i: (layer * batch + i // tiles, 0, 0)),
                  _resident((D_MODEL + XA_DIM, D_MODEL)), _resident((1, D_MODEL)),
                  _resident((D_MODEL, MLP_HIDDEN)), _resident((MLP_HIDDEN, D_MODEL)),
                  _resident((1, D_MODEL))],
        out_specs=row(D_MODEL),
        out_shape=jax.ShapeDtypeStruct((n, D_MODEL), F32),
        compiler_params=_params(("parallel",)),
        name="attn_mlp_final" if final else "attn_mlp",
    )(x2d, y_mix, xq, kv, w_out.astype(BF16), gain.reshape(1, D_MODEL), w_up.astype(BF16),
      w_down.astype(BF16), final_gain.reshape(1, D_MODEL))


def kernel(x, mem, norm_mix, norm_mem, w_kv, w_out, norm_mlp, w_up, w_down,
           gla_w_in, gla_w_gate2, gla_b_gate, gla_out_gain,
           hgrn_w_in, hgrn_lower_bounds, hgrn_out_gain, final_norm):
    batch, seq, _ = x.shape
    assert seq % MIX_TILE == 0 and seq % ROW_TILE == 0 and seq % ATTN_TILE == 0
    x2d = x.reshape(batch * seq, D_MODEL)
    kv = _memkv(mem.reshape(batch * MEM_LEN, D_MODEL), norm_mem, w_kv.astype(BF16))
    kv = kv.reshape(DEPTH * batch, MEM_LEN, 2 * XA_DIM)

    for i in range(DEPTH):
        j = i // 2
        if i % 2 == 0:
            q, k, v, ga, ghl, gate, xq = _proj_gla(x2d, norm_mix[i], gla_w_in[j], gla_w_gate2[j],
                                                   gla_b_gate[j])
            y_mix = _mixer(q, k, v, ga, ghl, gate, gla_out_gain[j],
                           batch=batch, heads=GLA_HEADS, dv=GLA_DV)
        else:
            q, k, v, ga, ghl, gate, xq = _proj_hgrn(x2d, norm_mix[i], hgrn_w_in[j],
                                                    hgrn_lower_bounds, i)
            y_mix = _mixer(q, k, v, ga, ghl, gate, hgrn_out_gain[j],
                           batch=batch, heads=HGRN_HEADS, dv=HGRN_DI)
        x2d = _attn_mlp(x2d, y_mix, xq, kv, w_out[i], norm_mlp[i], w_up[i], w_down[i], final_norm,
                        i, batch=batch, final=(i == DEPTH - 1))
    return x2d.reshape(batch, seq, D_MODEL)
```

```python
import functools

import jax
import jax.numpy as jnp
from jax import lax
from jax.experimental import pallas as pl
from jax.experimental.pallas import tpu as pltpu

F32 = jnp.float32
BF16 = jnp.bfloat16

D_MODEL = 1024
DEPTH = 4
MEM_LEN = 256
NORM_EPS = 1e-6
GLA_HEADS = 4
GLA_DK = 128
GLA_DV = 256
GLA_GATE_RANK = 16
GLA_TAU = 16.0
HGRN_HEADS = 8
HGRN_DF = 128
HGRN_DI = 128
XA_HEADS = 4
XA_DH = 128
XA_DIM = XA_HEADS * XA_DH
MLP_HIDDEN = 4 * D_MODEL

LANES = 128
SUBLANES = 8
V7X_VMEM_BYTES = 64 * 1024 * 1024
VMEM_LIMIT_BYTES = V7X_VMEM_BYTES - 8 * 1024 * 1024

CHUNK = 64
ROW_TILE = 512
ATTN_TILE = 1024
MIX_TILE = 4096
MIX_GROUP = 512
N_CHUNKS = MIX_TILE // CHUNK
GROUP_CHUNKS = MIX_GROUP // CHUNK
LEVELS = (1, 2, 4, 8, 16, 32)
MLP_HIDDEN_TILE = 1024
GATE_PAD = LANES
LOG2E = 1.4426950408889634

_NT = (((1,), (1,)), ((), ()))
_BNT = (((2,), (2,)), ((0,), (0,)))
_BNN = (((2,), (1,)), ((0,), (0,)))
_BTN = (((1,), (1,)), ((0,), (0,)))


def _params(semantics):
    return pltpu.CompilerParams(dimension_semantics=semantics, vmem_limit_bytes=VMEM_LIMIT_BYTES)


def _resident(shape):
    zeros = (0,) * len(shape)
    return pl.BlockSpec(shape, lambda *_: zeros, pipeline_mode=pl.Buffered(1))


def _rms_normed(x, gain):
    ms = jnp.mean(x * x, axis=-1, keepdims=True)
    return x * lax.rsqrt(ms + NORM_EPS) * gain


def _sigmoid(x):
    return 0.5 * jnp.tanh(0.5 * x) + 0.5


def _silu(x):
    half = 0.5 * x
    return half * jnp.tanh(half) + half


def _store_log2_decay(ga_ref, ghl_ref, log_decay, heads):
    g2 = log_decay * LOG2E
    ga_ref[...] = g2
    hi = g2.astype(BF16)
    lo = (g2 - hi.astype(F32)).astype(BF16)
    for h in range(heads):
        ghl_ref[:, (2 * h) * LANES:(2 * h + 1) * LANES] = hi[:, h * LANES:(h + 1) * LANES]
        ghl_ref[:, (2 * h + 1) * LANES:(2 * h + 2) * LANES] = lo[:, h * LANES:(h + 1) * LANES]


def _memkv_kernel(mem_ref, gain_ref, w_ref, o_ref):
    h = _rms_normed(mem_ref[...], gain_ref[0]).astype(BF16)
    o_ref[0] = jnp.dot(h, w_ref[0], preferred_element_type=F32).astype(BF16)


def _memkv(mem2d, norm_mem, w_kv):
    rows = mem2d.shape[0]
    return pl.pallas_call(
        _memkv_kernel,
        grid=(DEPTH,),
        in_specs=[
            _resident((rows, D_MODEL)),
            pl.BlockSpec((1, 1, D_MODEL), lambda l: (l, 0, 0)),
            pl.BlockSpec((1, D_MODEL, 2 * XA_DIM), lambda l: (l, 0, 0)),
        ],
        out_specs=pl.BlockSpec((1, rows, 2 * XA_DIM), lambda l: (l, 0, 0)),
        out_shape=jax.ShapeDtypeStruct((DEPTH, rows, 2 * XA_DIM), BF16),
        compiler_params=_params(("arbitrary",)),
        name="memkv",
    )(mem2d, norm_mem.reshape(DEPTH, 1, D_MODEL), w_kv)


_GLA_QKV = 2 * GLA_HEADS * GLA_DK + GLA_HEADS * GLA_DV
_GLA_RX = GLA_HEADS * GLA_DV + XA_DIM


def _proj_gla_kernel(x_ref, gain_ref, wqkv_ref, wg_ref, wrx_ref, wg2_ref, bg_ref,
                     q_ref, k_ref, v_ref, ga_ref, ghl_ref, r_ref, xq_ref):
    h = _rms_normed(x_ref[...], gain_ref[...]).astype(BF16)

    def mm(w_ref, a, b):
        return jnp.dot(h, w_ref[:, a:b], preferred_element_type=F32)

    q_ref[...] = (mm(wqkv_ref, 0, 512) * (GLA_DK ** -0.5)).astype(BF16)
    k_ref[...] = mm(wqkv_ref, 512, 1024).astype(BF16)
    v_ref[...] = mm(wqkv_ref, 1024, 2048).astype(BF16)
    g_lr = mm(wg_ref, 0, GATE_PAD).astype(BF16)
    z = jnp.dot(g_lr, wg2_ref[...], preferred_element_type=F32) + bg_ref[...]
    log_alpha = (jnp.minimum(z, 0.0) - jnp.log(1.0 + jnp.exp(-jnp.abs(z)))) * (1.0 / GLA_TAU)
    _store_log2_decay(ga_ref, ghl_ref, log_alpha, GLA_HEADS)
    r_ref[...] = _silu(mm(wrx_ref, 0, 1024)).astype(BF16)
    xq_ref[...] = mm(wrx_ref, 1024, _GLA_RX).astype(BF16)


def _proj_gla(x2d, gain, w_in, w_gate2, b_gate):
    n = x2d.shape[0]
    g0, g1 = _GLA_QKV, _GLA_QKV + GLA_GATE_RANK
    w_qkv = w_in[:, :g0].astype(BF16)
    w_g = jnp.pad(w_in[:, g0:g1], ((0, 0), (0, GATE_PAD - GLA_GATE_RANK))).astype(BF16)
    w_rx = w_in[:, g1:].astype(BF16)
    wg2 = jnp.pad(w_gate2, ((0, GATE_PAD - GLA_GATE_RANK), (0, 0))).astype(BF16)
    row = lambda width: pl.BlockSpec((ROW_TILE, width), lambda i: (i, 0))
    out = lambda width, dt: jax.ShapeDtypeStruct((n, width), dt)
    return pl.pallas_call(
        _proj_gla_kernel,
        grid=(n // ROW_TILE,),
        in_specs=[row(D_MODEL), _resident((1, D_MODEL)), _resident((D_MODEL, _GLA_QKV)),
                  _resident((D_MODEL, GATE_PAD)), _resident((D_MODEL, _GLA_RX)),
                  _resident((GATE_PAD, 512)), _resident((1, 512))],
        out_specs=[row(512), row(512), row(1024), row(512), row(1024), row(1024), row(XA_DIM)],
        out_shape=[out(512, BF16), out(512, BF16), out(1024, BF16), out(512, F32),
                   out(1024, BF16), out(1024, BF16), out(XA_DIM, BF16)],
        compiler_params=_params(("parallel",)),
        name="proj_gla",
    )(x2d, gain.reshape(1, D_MODEL), w_qkv, w_g, w_rx, wg2, b_gate.reshape(1, 512))


def _proj_hgrn_kernel(x_ref, gain_ref, w_ref, lbraw_ref,
                      q_ref, k_ref, v_ref, ga_ref, ghl_ref, g_ref, xq_ref, *, layer):
    h = _rms_normed(x_ref[...], gain_ref[...]).astype(BF16)

    def mm(a, b):
        return jnp.dot(h, w_ref[:, a:b], preferred_element_type=F32)

    raw = lbraw_ref[...]
    e = jnp.exp(raw - jnp.max(raw, axis=0, keepdims=True))
    p = e / jnp.sum(e, axis=0, keepdims=True)
    lb = jnp.sum(p[1:layer + 1, :], axis=0, keepdims=True)

    q_ref[...] = (_silu(mm(0, 1024)) * (HGRN_DF ** -0.5)).astype(BF16)
    forget = lb + (1.0 - lb) * _sigmoid(mm(1024, 2048))
    k_ref[...] = (1.0 - forget).astype(BF16)
    _store_log2_decay(ga_ref, ghl_ref, jnp.log(forget), HGRN_HEADS)
    v_ref[...] = mm(2048, 3072).astype(BF16)
    g_ref[...] = _silu(mm(3072, 4096)).astype(BF16)
    xq_ref[...] = mm(4096, 4608).astype(BF16)


def _proj_hgrn(x2d, gain, w_in, lower_bounds_raw, layer):
    n = x2d.shape[0]
    row = lambda width: pl.BlockSpec((ROW_TILE, width), lambda i: (i, 0))
    out = lambda width, dt: jax.ShapeDtypeStruct((n, width), dt)
    return pl.pallas_call(
        functools.partial(_proj_hgrn_kernel, layer=layer),
        grid=(n // ROW_TILE,),
        in_specs=[row(D_MODEL), _resident((1, D_MODEL)), _resident((D_MODEL, 4608)),
                  _resident((DEPTH, 1024))],
        out_specs=[row(1024), row(1024), row(1024), row(1024), row(2048), row(1024), row(XA_DIM)],
        out_shape=[out(1024, BF16), out(1024, BF16), out(1024, BF16), out(1024, F32),
                   out(2048, BF16), out(1024, BF16), out(XA_DIM, BF16)],
        compiler_params=_params(("parallel",)),
        name="proj_hgrn",
    )(x2d, gain.reshape(1, D_MODEL), w_in.astype(BF16), lower_bounds_raw)


def _level_rows(b, g2, q, k, n):
    if n >= SUBLANES:
        split = lambda a: a.reshape(MIX_GROUP // (2 * n), 2, n, LANES)
        b4 = split(b)
        r = b4[:, 0, n - 1:n, :]
        lower = split(k)[:, 0] * jnp.exp2(r - b4[:, 0])
        upper = split(q)[:, 1] * jnp.exp2(b4[:, 1] - r)
        return jnp.stack([lower, upper], axis=1).reshape(MIX_GROUP, LANES)
    tile = lambda a: a.reshape(MIX_GROUP // SUBLANES, SUBLANES, LANES)
    sub = lax.broadcasted_iota(jnp.int32, (1, SUBLANES, LANES), 1)
    is_upper = (sub & n) != 0
    if n == 1:
        e = jnp.where(is_upper, tile(g2), 0.0)
    else:
        b3 = tile(b)
        r = b3[:, n - 1:n, :]
        for start in range(2 * n, SUBLANES, 2 * n):
            r = jnp.where(sub >= start, b3[:, start + n - 1:start + n, :], r)
        d = b3 - r
        e = jnp.where(is_upper, d, -d)
    x = jnp.where(is_upper, tile(q), tile(k)) * jnp.exp2(e)
    return x.reshape(MIX_GROUP, LANES)


def _mixer_kernel(q_ref, k_ref, v_ref, ga_ref, ghl_ref, gate_ref, gain_ref, o_ref, s_ref, sall_ref,
                  *, dv):
    @pl.when(pl.program_id(2) == 0)
    def _():
        s_ref[...] = jnp.zeros_like(s_ref)

    def tri_masks(n_chunks):
        shape3 = (n_chunks, CHUNK, CHUNK)
        rows = lax.broadcasted_iota(jnp.int32, shape3, 1)
        cols = lax.broadcasted_iota(jnp.int32, shape3, 2)
        return rows >= cols, rows ^ cols

    causal_all, _ = tri_masks(N_CHUNKS)
    bb = lax.dot_general(causal_all.astype(BF16), ghl_ref[...].reshape(N_CHUNKS, CHUNK, 2 * LANES),
                         _BNN, preferred_element_type=F32)
    b_all = (bb[:, :, :LANES] + bb[:, :, LANES:]).reshape(MIX_TILE, LANES)

    causal, sep = tri_masks(1)

    def chunks(a):
        return a.reshape(GROUP_CHUNKS, CHUNK, a.shape[-1])

    s = s_ref[...]
    for group in range(MIX_TILE // MIX_GROUP):
        lo = group * MIX_GROUP
        rs = pl.ds(lo, MIX_GROUP)
        sall = sall_ref.at[pl.ds(group * GROUP_CHUNKS, GROUP_CHUNKS)]
        b = b_all[lo:lo + MIX_GROUP, :]
        g2 = ga_ref[rs, :]
        qb = q_ref[rs, :]
        kb = k_ref[rs, :]
        v = chunks(v_ref[rs, :])
        q = qb.astype(F32)
        k = kb.astype(F32)

        attn = jnp.broadcast_to(jnp.sum(chunks(q * k), axis=-1, keepdims=True),
                                (GROUP_CHUNKS, CHUNK, CHUNK))
        for n in LEVELS:
            x = chunks(_level_rows(b, g2, q, k, n).astype(BF16))
            p = lax.dot_general(x, x, _BNT, preferred_element_type=F32)
            attn = jnp.where(sep >= n, p, attn)
        attn = jnp.where(causal, attn, 0.0).astype(BF16)

        b3 = chunks(b)
        b_last = b3[:, CHUNK - 1:CHUNK, :]
        qd = (chunks(q) * jnp.exp2(b3)).astype(BF16)
        kd = (chunks(k) * jnp.exp2(b_last - b3)).astype(BF16)
        u = lax.dot_general(kd, v, _BTN, preferred_element_type=F32)
        decay = jnp.swapaxes(jnp.broadcast_to(jnp.exp2(b_last), (GROUP_CHUNKS, LANES, LANES)), 1, 2)
        if dv > LANES:
            decay = jnp.concatenate([decay] * (dv // LANES), axis=2)

        for c in range(GROUP_CHUNKS):
            sall[c] = s.astype(BF16)
            s = decay[c] * s + u[c]

        o = (lax.dot_general(attn, v, _BNN, preferred_element_type=F32)
             + lax.dot_general(qd, sall[...], _BNN, preferred_element_type=F32))
        y = _rms_normed(o.reshape(MIX_GROUP, dv), gain_ref[...]) * gate_ref[rs, :].astype(F32)
        o_ref[rs, :] = y.astype(o_ref.dtype)
    s_ref[...] = s


def _mixer(q, k, v, ga, ghl, gate, out_gain, *, batch, heads, dv):
    n = q.shape[0]
    tiles = n // batch // MIX_TILE
    kspec = pl.BlockSpec((MIX_TILE, LANES), lambda b, h, t: (b * tiles + t, h))
    hlspec = pl.BlockSpec((MIX_TILE, 2 * LANES), lambda b, h, t: (b * tiles + t, h))
    vspec = pl.BlockSpec((MIX_TILE, dv), lambda b, h, t: (b * tiles + t, h))
    return pl.pallas_call(
        functools.partial(_mixer_kernel, dv=dv),
        grid=(batch, heads, tiles),
        in_specs=[kspec, kspec, vspec, kspec, hlspec, vspec, _resident((1, dv))],
        out_specs=vspec,
        out_shape=jax.ShapeDtypeStruct((n, D_MODEL), BF16),
        scratch_shapes=[pltpu.VMEM((LANES, dv), F32), pltpu.VMEM((N_CHUNKS, LANES, dv), BF16)],
        compiler_params=_params(("parallel", "parallel", "arbitrary")),
        name=f"mixer_dv{dv}",
    )(q, k, v, ga, ghl, gate, out_gain.reshape(1, dv))


def _attn_mlp_kernel(x_ref, ymix_ref, xq_ref, kv_ref, wout_ref, gain_ref, wup_ref, wdn_ref,
                     fgain_ref, o_ref, *, final):
    acc = x_ref[...] + jnp.dot(ymix_ref[...], wout_ref[0:D_MODEL, :], preferred_element_type=F32)
    xq = xq_ref[...]
    kv = kv_ref[0]
    heads = []
    for h in range(XA_HEADS):
        lo, hi = h * XA_DH, (h + 1) * XA_DH
        s = lax.dot_general(xq[:, lo:hi], kv[:, lo:hi], _NT, preferred_element_type=F32)
        s = s * (XA_DH ** -0.5)
        e = jnp.exp(s - jnp.max(s, axis=-1, keepdims=True)).astype(BF16)
        vh = kv[:, XA_DIM + lo:XA_DIM + hi]
        num_den = jnp.dot(e, jnp.concatenate([vh, jnp.ones_like(vh)], axis=1),
                          preferred_element_type=F32)
        heads.append((num_den[:, :XA_DH] / num_den[:, XA_DH:]).astype(BF16))
    y_mem = jnp.concatenate(heads, axis=1)
    x = acc + jnp.dot(y_mem, wout_ref[D_MODEL:, :], preferred_element_type=F32)

    h = _rms_normed(x, gain_ref[...]).astype(BF16)
    mlp = None
    for c in range(0, MLP_HIDDEN, MLP_HIDDEN_TILE):
        u = jnp.maximum(jnp.dot(h, wup_ref[:, c:c + MLP_HIDDEN_TILE], preferred_element_type=F32), 0.0)
        d = jnp.dot((u * u).astype(BF16), wdn_ref[c:c + MLP_HIDDEN_TILE, :],
                    preferred_element_type=F32)
        mlp = d if mlp is None else mlp + d
    out = x + mlp
    o_ref[...] = _rms_normed(out, fgain_ref[...]) if final else out


def _attn_mlp(x2d, y_mix, xq, kv, w_out, gain, w_up, w_down, final_gain, layer, *, batch, final):
    n = x2d.shape[0]
    tiles = n // batch // ATTN_TILE
    row = lambda width: pl.BlockSpec((ATTN_TILE, width), lambda i: (i, 0))
    return pl.pallas_call(
        functools.partial(_attn_mlp_kernel, final=final),
        grid=(n // ATTN_TILE,),
        in_specs=[row(D_MODEL), row(D_MODEL), row(XA_DIM),
                  pl.BlockSpec((1, MEM_LEN, 2 * XA_DIM), lambda i: (layer * batch + i // tiles, 0, 0)),
                  _resident((D_MODEL + XA_DIM, D_MODEL)), _resident((1, D_MODEL)),
                  _resident((D_MODEL, MLP_HIDDEN)), _resident((MLP_HIDDEN, D_MODEL)),
                  _resident((1, D_MODEL))],
        out_specs=row(D_MODEL),
        out_shape=jax.ShapeDtypeStruct((n, D_MODEL), F32),
        compiler_params=_params(("parallel",)),
        name="attn_mlp_final" if final else "attn_mlp",
    )(x2d, y_mix, xq, kv, w_out.astype(BF16), gain.reshape(1, D_MODEL), w_up.astype(BF16),
      w_down.astype(BF16), final_gain.reshape(1, D_MODEL))


def kernel(x, mem, norm_mix, norm_mem, w_kv, w_out, norm_mlp, w_up, w_down,
           gla_w_in, gla_w_gate2, gla_b_gate, gla_out_gain,
           hgrn_w_in, hgrn_lower_bounds, hgrn_out_gain, final_norm):
    batch, seq, _ = x.shape
    assert seq % MIX_TILE == 0 and seq % ROW_TILE == 0 and seq % ATTN_TILE == 0
    x2d = x.reshape(batch * seq, D_MODEL)
    kv = _memkv(mem.reshape(batch * MEM_LEN, D_MODEL), norm_mem, w_kv.astype(BF16))
    kv = kv.reshape(DEPTH * batch, MEM_LEN, 2 * XA_DIM)

    for i in range(DEPTH):
        j = i // 2
        if i % 2 == 0:
            q, k, v, ga, ghl, gate, xq = _proj_gla(x2d, norm_mix[i], gla_w_in[j], gla_w_gate2[j],
                                                   gla_b_gate[j])
            y_mix = _mixer(q, k, v, ga, ghl, gate, gla_out_gain[j],
                           batch=batch, heads=GLA_HEADS, dv=GLA_DV)
        else:
            q, k, v, ga, ghl, gate, xq = _proj_hgrn(x2d, norm_mix[i], hgrn_w_in[j],
                                                    hgrn_lower_bounds, i)
            y_mix = _mixer(q, k, v, ga, ghl, gate, hgrn_out_gain[j],
                           batch=batch, heads=HGRN_HEADS, dv=HGRN_DI)
        x2d = _attn_mlp(x2d, y_mix, xq, kv, w_out[i], norm_mlp[i], w_up[i], w_down[i], final_norm,
                        i, batch=batch, final=(i == DEPTH - 1))
    return x2d.reshape(batch, seq, D_MODEL)
```

```python
import functools

import jax
import jax.numpy as jnp
from jax import lax
from jax.experimental import pallas as pl
from jax.experimental.pallas import tpu as pltpu

F32 = jnp.float32
BF16 = jnp.bfloat16

D_MODEL = 1024
DEPTH = 4
MEM_LEN = 256
NORM_EPS = 1e-6
GLA_HEADS = 4
GLA_DK = 128
GLA_DV = 256
GLA_GATE_RANK = 16
GLA_TAU = 16.0
HGRN_HEADS = 8
HGRN_DF = 128
HGRN_DI = 128
XA_HEADS = 4
XA_DH = 128
XA_DIM = XA_HEADS * XA_DH
MLP_HIDDEN = 4 * D_MODEL

LANES = 128
SUBLANES = 8
V7X_VMEM_BYTES = 64 * 1024 * 1024
VMEM_LIMIT_BYTES = V7X_VMEM_BYTES - 8 * 1024 * 1024

CHUNK = 64
ROW_TILE = 512
ATTN_TILE = 1024
MIX_TILE = 4096
MIX_GROUP = 512
N_CHUNKS = MIX_TILE // CHUNK
GROUP_CHUNKS = MIX_GROUP // CHUNK
LEVELS = (1, 2, 4, 8, 16, 32)
MLP_HIDDEN_TILE = 1024
GATE_PAD = LANES
LOG2E = 1.4426950408889634

_NT = (((1,), (1,)), ((), ()))
_BNT = (((2,), (2,)), ((0,), (0,)))
_BNN = (((2,), (1,)), ((0,), (0,)))
_BTN = (((1,), (1,)), ((0,), (0,)))


def _params(semantics):
    return pltpu.CompilerParams(dimension_semantics=semantics, vmem_limit_bytes=VMEM_LIMIT_BYTES)


def _resident(shape):
    zeros = (0,) * len(shape)
    return pl.BlockSpec(shape, lambda *_: zeros, pipeline_mode=pl.Buffered(1))


def _rms_normed(x, gain):
    ms = jnp.mean(x * x, axis=-1, keepdims=True)
    return x * lax.rsqrt(ms + NORM_EPS) * gain


def _sigmoid(x):
    return 0.5 * jnp.tanh(0.5 * x) + 0.5


def _silu(x):
    half = 0.5 * x
    return half * jnp.tanh(half) + half


def _slab_columns(dv):
    q, k, v = 0, LANES, 2 * LANES
    hi = v + dv
    lo = hi + LANES
    gate = lo + LANES
    return q, k, v, hi, lo, gate, gate + dv


def _store_head_slabs(slab_ref, q, k, v, log_decay, gate, dv):
    g2 = log_decay * LOG2E
    hi = g2.astype(BF16)
    lo = (g2 - hi.astype(F32)).astype(BF16)
    cq, ck, cv, chi, clo, cgate, _ = _slab_columns(dv)
    for h in range(slab_ref.shape[0]):
        dk_cols = slice(h * LANES, (h + 1) * LANES)
        dv_cols = slice(h * dv, (h + 1) * dv)
        slab_ref[h, :, cq:cq + LANES] = q[:, dk_cols].astype(BF16)
        slab_ref[h, :, ck:ck + LANES] = k[:, dk_cols].astype(BF16)
        slab_ref[h, :, cv:cv + dv] = v[:, dv_cols].astype(BF16)
        slab_ref[h, :, chi:chi + LANES] = hi[:, dk_cols]
        slab_ref[h, :, clo:clo + LANES] = lo[:, dk_cols]
        slab_ref[h, :, cgate:cgate + dv] = gate[:, dv_cols].astype(BF16)


def _memkv_kernel(mem_ref, gain_ref, w_ref, o_ref):
    h = _rms_normed(mem_ref[...], gain_ref[0]).astype(BF16)
    o_ref[0] = jnp.dot(h, w_ref[0], preferred_element_type=F32).astype(BF16)


def _memkv(mem2d, norm_mem, w_kv):
    rows = mem2d.shape[0]
    return pl.pallas_call(
        _memkv_kernel,
        grid=(DEPTH,),
        in_specs=[
            _resident((rows, D_MODEL)),
            pl.BlockSpec((1, 1, D_MODEL), lambda l: (l, 0, 0)),
            pl.BlockSpec((1, D_MODEL, 2 * XA_DIM), lambda l: (l, 0, 0)),
        ],
        out_specs=pl.BlockSpec((1, rows, 2 * XA_DIM), lambda l: (l, 0, 0)),
        out_shape=jax.ShapeDtypeStruct((DEPTH, rows, 2 * XA_DIM), BF16),
        compiler_params=_params(("arbitrary",)),
        name="memkv",
    )(mem2d, norm_mem.reshape(DEPTH, 1, D_MODEL), w_kv)


def _proj_out_specs(heads, dv):
    width = _slab_columns(dv)[-1]
    return [pl.BlockSpec((heads, ROW_TILE, width), lambda i: (0, i, 0)),
            pl.BlockSpec((ROW_TILE, XA_DIM), lambda i: (i, 0))]


def _proj_out_shapes(n, heads, dv):
    width = _slab_columns(dv)[-1]
    return [jax.ShapeDtypeStruct((heads, n, width), BF16), jax.ShapeDtypeStruct((n, XA_DIM), BF16)]


_GLA_QKV = 2 * GLA_HEADS * GLA_DK + GLA_HEADS * GLA_DV
_GLA_RX = GLA_HEADS * GLA_DV + XA_DIM


def _proj_gla_kernel(x_ref, gain_ref, wqkv_ref, wg_ref, wrx_ref, wg2_ref, bg_ref,
                     slab_ref, xq_ref):
    h = _rms_normed(x_ref[...], gain_ref[...]).astype(BF16)

    def mm(w_ref, a, b):
        return jnp.dot(h, w_ref[:, a:b], preferred_element_type=F32)

    q = mm(wqkv_ref, 0, 512) * (GLA_DK ** -0.5)
    k = mm(wqkv_ref, 512, 1024)
    v = mm(wqkv_ref, 1024, 2048)
    g_lr = mm(wg_ref, 0, GATE_PAD).astype(BF16)
    z = jnp.dot(g_lr, wg2_ref[...], preferred_element_type=F32) + bg_ref[...]
    log_alpha = (jnp.minimum(z, 0.0) - jnp.log(1.0 + jnp.exp(-jnp.abs(z)))) * (1.0 / GLA_TAU)
    gate = _silu(mm(wrx_ref, 0, 1024))
    _store_head_slabs(slab_ref, q, k, v, log_alpha, gate, GLA_DV)
    xq_ref[...] = mm(wrx_ref, 1024, _GLA_RX).astype(BF16)


def _proj_gla(x2d, gain, w_in, w_gate2, b_gate):
    n = x2d.shape[0]
    g0, g1 = _GLA_QKV, _GLA_QKV + GLA_GATE_RANK
    w_qkv = w_in[:, :g0].astype(BF16)
    w_g = jnp.pad(w_in[:, g0:g1], ((0, 0), (0, GATE_PAD - GLA_GATE_RANK))).astype(BF16)
    w_rx = w_in[:, g1:].astype(BF16)
    wg2 = jnp.pad(w_gate2, ((0, GATE_PAD - GLA_GATE_RANK), (0, 0))).astype(BF16)
    row = lambda width: pl.BlockSpec((ROW_TILE, width), lambda i: (i, 0))
    return pl.pallas_call(
        _proj_gla_kernel,
        grid=(n // ROW_TILE,),
        in_specs=[row(D_MODEL), _resident((1, D_MODEL)), _resident((D_MODEL, _GLA_QKV)),
                  _resident((D_MODEL, GATE_PAD)), _resident((D_MODEL, _GLA_RX)),
                  _resident((GATE_PAD, 512)), _resident((1, 512))],
        out_specs=_proj_out_specs(GLA_HEADS, GLA_DV),
        out_shape=_proj_out_shapes(n, GLA_HEADS, GLA_DV),
        compiler_params=_params(("parallel",)),
        name="proj_gla",
    )(x2d, gain.reshape(1, D_MODEL), w_qkv, w_g, w_rx, wg2, b_gate.reshape(1, 512))


def _proj_hgrn_kernel(x_ref, gain_ref, w_ref, lbraw_ref,
                      slab_ref, xq_ref, *, layer):
    h = _rms_normed(x_ref[...], gain_ref[...]).astype(BF16)

    def mm(a, b):
        return jnp.dot(h, w_ref[:, a:b], preferred_element_type=F32)

    raw = lbraw_ref[...]
    e = jnp.exp(raw - jnp.max(raw, axis=0, keepdims=True))
    p = e / jnp.sum(e, axis=0, keepdims=True)
    lb = jnp.sum(p[1:layer + 1, :], axis=0, keepdims=True)

    q = _silu(mm(0, 1024)) * (HGRN_DF ** -0.5)
    forget = lb + (1.0 - lb) * _sigmoid(mm(1024, 2048))
    v = mm(2048, 3072)
    gate = _silu(mm(3072, 4096))
    _store_head_slabs(slab_ref, q, 1.0 - forget, v, jnp.log(forget), gate, HGRN_DI)
    xq_ref[...] = mm(4096, 4608).astype(BF16)


def _proj_hgrn(x2d, gain, w_in, lower_bounds_raw, layer):
    n = x2d.shape[0]
    return pl.pallas_call(
        functools.partial(_proj_hgrn_kernel, layer=layer),
        grid=(n // ROW_TILE,),
        in_specs=[pl.BlockSpec((ROW_TILE, D_MODEL), lambda i: (i, 0)), _resident((1, D_MODEL)),
                  _resident((D_MODEL, 4608)), _resident((DEPTH, 1024))],
        out_specs=_proj_out_specs(HGRN_HEADS, HGRN_DI),
        out_shape=_proj_out_shapes(n, HGRN_HEADS, HGRN_DI),
        compiler_params=_params(("parallel",)),
        name="proj_hgrn",
    )(x2d, gain.reshape(1, D_MODEL), w_in.astype(BF16), lower_bounds_raw)


def _level_rows(b, g2, q, k, n):
    if n >= SUBLANES:
        split = lambda a: a.reshape(MIX_GROUP // (2 * n), 2, n, LANES)
        b4 = split(b)
        r = b4[:, 0, n - 1:n, :]
        lower = split(k)[:, 0] * jnp.exp2(r - b4[:, 0])
        upper = split(q)[:, 1] * jnp.exp2(b4[:, 1] - r)
        return jnp.stack([lower, upper], axis=1).reshape(MIX_GROUP, LANES)
    tile = lambda a: a.reshape(MIX_GROUP // SUBLANES, SUBLANES, LANES)
    sub = lax.broadcasted_iota(jnp.int32, (1, SUBLANES, LANES), 1)
    is_upper = (sub & n) != 0
    if n == 1:
        e = jnp.where(is_upper, tile(g2), 0.0)
    else:
        b3 = tile(b)
        r = b3[:, n - 1:n, :]
        for start in range(2 * n, SUBLANES, 2 * n):
            r = jnp.where(sub >= start, b3[:, start + n - 1:start + n, :], r)
        d = b3 - r
        e = jnp.where(is_upper, d, -d)
    x = jnp.where(is_upper, tile(q), tile(k)) * jnp.exp2(e)
    return x.reshape(MIX_GROUP, LANES)


def _mixer_kernel(slab_ref, gain_ref, o_ref, s_ref, sall_ref, *, dv):
    @pl.when(pl.program_id(2) == 0)
    def _():
        s_ref[...] = jnp.zeros_like(s_ref)

    def tri_masks(n_chunks):
        shape3 = (n_chunks, CHUNK, CHUNK)
        rows = lax.broadcasted_iota(jnp.int32, shape3, 1)
        cols = lax.broadcasted_iota(jnp.int32, shape3, 2)
        return rows >= cols, rows ^ cols

    cq, ck, cv, chi, clo, cgate, _ = _slab_columns(dv)
    causal_all, _ = tri_masks(N_CHUNKS)
    hi_lo = slab_ref[0, :, chi:chi + 2 * LANES]
    bb = lax.dot_general(causal_all.astype(BF16), hi_lo.reshape(N_CHUNKS, CHUNK, 2 * LANES),
                         _BNN, preferred_element_type=F32)
    b_all = (bb[:, :, :LANES] + bb[:, :, LANES:]).reshape(MIX_TILE, LANES)

    causal, sep = tri_masks(1)

    def chunks(a):
        return a.reshape(GROUP_CHUNKS, CHUNK, a.shape[-1])

    s = s_ref[...]
    for group in range(MIX_TILE // MIX_GROUP):
        lo = group * MIX_GROUP
        rs = pl.ds(lo, MIX_GROUP)
        sall = sall_ref.at[pl.ds(group * GROUP_CHUNKS, GROUP_CHUNKS)]
        b = b_all[lo:lo + MIX_GROUP, :]
        g2 = (slab_ref[0, rs, chi:chi + LANES].astype(F32)
              + slab_ref[0, rs, clo:clo + LANES].astype(F32))
        q = slab_ref[0, rs, cq:cq + LANES].astype(F32)
        k = slab_ref[0, rs, ck:ck + LANES].astype(F32)
        v = chunks(slab_ref[0, rs, cv:cv + dv])

        attn = jnp.broadcast_to(jnp.sum(chunks(q * k), axis=-1, keepdims=True),
                                (GROUP_CHUNKS, CHUNK, CHUNK))
        for n in LEVELS:
            x = chunks(_level_rows(b, g2, q, k, n).astype(BF16))
            p = lax.dot_general(x, x, _BNT, preferred_element_type=F32)
            attn = jnp.where(sep >= n, p, attn)
        attn = jnp.where(causal, attn, 0.0).astype(BF16)

        b3 = chunks(b)
        b_last = b3[:, CHUNK - 1:CHUNK, :]
        qd = (chunks(q) * jnp.exp2(b3)).astype(BF16)
        kd = (chunks(k) * jnp.exp2(b_last - b3)).astype(BF16)
        u = lax.dot_general(kd, v, _BTN, preferred_element_type=F32)
        decay = jnp.swapaxes(jnp.broadcast_to(jnp.exp2(b_last), (GROUP_CHUNKS, LANES, LANES)), 1, 2)
        if dv > LANES:
            decay = jnp.concatenate([decay] * (dv // LANES), axis=2)

        for c in range(GROUP_CHUNKS):
            sall[c] = s.astype(BF16)
            s = decay[c] * s + u[c]

        o = (lax.dot_general(attn, v, _BNN, preferred_element_type=F32)
             + lax.dot_general(qd, sall[...], _BNN, preferred_element_type=F32))
        gate = slab_ref[0, rs, cgate:cgate + dv].astype(F32)
        y = _rms_normed(o.reshape(MIX_GROUP, dv), gain_ref[...]) * gate
        o_ref[0, rs, :] = y.astype(o_ref.dtype)
    s_ref[...] = s


def _mixer(slab, out_gain, *, batch):
    heads, n, width = slab.shape
    dv = out_gain.shape[0]
    tiles = n // batch // MIX_TILE
    block = lambda cols: pl.BlockSpec((1, MIX_TILE, cols), lambda b, h, t: (h, b * tiles + t, 0))
    return pl.pallas_call(
        functools.partial(_mixer_kernel, dv=dv),
        grid=(batch, heads, tiles),
        in_specs=[block(width), _resident((1, dv))],
        out_specs=block(dv),
        out_shape=jax.ShapeDtypeStruct((heads, n, dv), BF16),
        scratch_shapes=[pltpu.VMEM((LANES, dv), F32), pltpu.VMEM((N_CHUNKS, LANES, dv), BF16)],
        compiler_params=_params(("parallel", "parallel", "arbitrary")),
        name=f"mixer_dv{dv}",
    )(slab, out_gain.reshape(1, dv))


def _attn_mlp_kernel(x_ref, ymix_ref, xq_ref, kv_ref, wout_ref, gain_ref, wup_ref, wdn_ref,
                     fgain_ref, o_ref, *, final):
    y_mix = jnp.concatenate([ymix_ref[h] for h in range(ymix_ref.shape[0])], axis=1)
    acc = x_ref[...] + jnp.dot(y_mix, wout_ref[0:D_MODEL, :], preferred_element_type=F32)
    xq = xq_ref[...]
    kv = kv_ref[0]
    heads = []
    for h in range(XA_HEADS):
        lo, hi = h * XA_DH, (h + 1) * XA_DH
        s = lax.dot_general(xq[:, lo:hi], kv[:, lo:hi], _NT, preferred_element_type=F32)
        s = s * (XA_DH ** -0.5)
        e = jnp.exp(s - jnp.max(s, axis=-1, keepdims=True)).astype(BF16)
        vh = kv[:, XA_DIM + lo:XA_DIM + hi]
        num_den = jnp.dot(e, jnp.concatenate([vh, jnp.ones_like(vh)], axis=1),
                          preferred_element_type=F32)
        heads.append((num_den[:, :XA_DH] / num_den[:, XA_DH:]).astype(BF16))
    y_mem = jnp.concatenate(heads, axis=1)
    x = acc + jnp.dot(y_mem, wout_ref[D_MODEL:, :], preferred_element_type=F32)

    h = _rms_normed(x, gain_ref[...]).astype(BF16)
    mlp = None
    for c in range(0, MLP_HIDDEN, MLP_HIDDEN_TILE):
        u = jnp.maximum(jnp.dot(h, wup_ref[:, c:c + MLP_HIDDEN_TILE], preferred_element_type=F32), 0.0)
        d = jnp.dot((u * u).astype(BF16), wdn_ref[c:c + MLP_HIDDEN_TILE, :],
                    preferred_element_type=F32)
        mlp = d if mlp is None else mlp + d
    out = x + mlp
    o_ref[...] = _rms_normed(out, fgain_ref[...]) if final else out


def _attn_mlp(x2d, y_mix, xq, kv, w_out, gain, w_up, w_down, final_gain, layer, *, batch, final):
    n = x2d.shape[0]
    tiles = n // batch // ATTN_TILE
    row = lambda width: pl.BlockSpec((ATTN_TILE, width), lambda i: (i, 0))
    heads, _, dv = y_mix.shape
    return pl.pallas_call(
        functools.partial(_attn_mlp_kernel, final=final),
        grid=(n // ATTN_TILE,),
        in_specs=[row(D_MODEL), pl.BlockSpec((heads, ATTN_TILE, dv), lambda i: (0, i, 0)),
                  row(XA_DIM),
                  pl.BlockSpec((1, MEM_LEN, 2 * XA_DIM), lambda i: (layer * batch + i // tiles, 0, 0)),
                  _resident((D_MODEL + XA_DIM, D_MODEL)), _resident((1, D_MODEL)),
                  _resident((D_MODEL, MLP_HIDDEN)), _resident((MLP_HIDDEN, D_MODEL)),
                  _resident((1, D_MODEL))],
        out_specs=row(D_MODEL),
        out_shape=jax.ShapeDtypeStruct((n, D_MODEL), F32),
        compiler_params=_params(("parallel",)),
        name="attn_mlp_final" if final else "attn_mlp",
    )(x2d, y_mix, xq, kv, w_out.astype(BF16), gain.reshape(1, D_MODEL), w_up.astype(BF16),
      w_down.astype(BF16), final_gain.reshape(1, D_MODEL))


def kernel(x, mem, norm_mix, norm_mem, w_kv, w_out, norm_mlp, w_up, w_down,
           gla_w_in, gla_w_gate2, gla_b_gate, gla_out_gain,
           hgrn_w_in, hgrn_lower_bounds, hgrn_out_gain, final_norm):
    batch, seq, _ = x.shape
    assert seq % MIX_TILE == 0 and seq % ROW_TILE == 0 and seq % ATTN_TILE == 0
    x2d = x.reshape(batch * seq, D_MODEL)
    kv = _memkv(mem.reshape(batch * MEM_LEN, D_MODEL), norm_mem, w_kv.astype(BF16))
    kv = kv.reshape(DEPTH * batch, MEM_LEN, 2 * XA_DIM)

    for i in range(DEPTH):
        j = i // 2
        if i % 2 == 0:
            slab, xq = _proj_gla(x2d, norm_mix[i], gla_w_in[j], gla_w_gate2[j], gla_b_gate[j])
            y_mix = _mixer(slab, gla_out_gain[j], batch=batch)
        else:
            slab, xq = _proj_hgrn(x2d, norm_mix[i], hgrn_w_in[j], hgrn_lower_bounds, i)
            y_mix = _mixer(slab, hgrn_out_gain[j], batch=batch)
        x2d = _attn_mlp(x2d, y_mix, xq, kv, w_out[i], norm_mlp[i], w_up[i], w_down[i], final_norm,
                        i, batch=batch, final=(i == DEPTH - 1))
    return x2d.reshape(batch, seq, D_MODEL)
```

```python
import functools

import jax
import jax.numpy as jnp
from jax import lax
from jax.experimental import pallas as pl
from jax.experimental.pallas import tpu as pltpu

F32 = jnp.float32
BF16 = jnp.bfloat16

D_MODEL = 1024
DEPTH = 4
MEM_LEN = 256
NORM_EPS = 1e-6
GLA_HEADS = 4
GLA_DK = 128
GLA_DV = 256
GLA_GATE_RANK = 16
GLA_TAU = 16.0
HGRN_HEADS = 8
HGRN_DF = 128
HGRN_DI = 128
XA_HEADS = 4
XA_DH = 128
XA_DIM = XA_HEADS * XA_DH
MLP_HIDDEN = 4 * D_MODEL

LANES = 128
SUBLANES = 8
V7X_VMEM_BYTES = 64 * 1024 * 1024
VMEM_LIMIT_BYTES = V7X_VMEM_BYTES - 8 * 1024 * 1024

CHUNK = 64
ROW_TILE = 1024
ATTN_TILE = 1024
MIX_TILE = 4096
MIX_GROUP = 512
N_CHUNKS = MIX_TILE // CHUNK
GROUP_CHUNKS = MIX_GROUP // CHUNK
LEVELS = (1, 2, 4, 8, 16, 32)
MLP_HIDDEN_TILE = 1024
GATE_PAD = LANES
LOG2E = 1.4426950408889634

_NT = (((1,), (1,)), ((), ()))
_BNT = (((2,), (2,)), ((0,), (0,)))
_BNN = (((2,), (1,)), ((0,), (0,)))
_BTN = (((1,), (1,)), ((0,), (0,)))


def _params(semantics):
    return pltpu.CompilerParams(dimension_semantics=semantics, vmem_limit_bytes=VMEM_LIMIT_BYTES)


def _resident(shape):
    zeros = (0,) * len(shape)
    return pl.BlockSpec(shape, lambda *_: zeros, pipeline_mode=pl.Buffered(1))


def _rms_normed(x, gain):
    ms = jnp.mean(x * x, axis=-1, keepdims=True)
    return x * lax.rsqrt(ms + NORM_EPS) * gain


def _sigmoid(x):
    return 0.5 * jnp.tanh(0.5 * x) + 0.5


def _silu(x):
    half = 0.5 * x
    return half * jnp.tanh(half) + half


def _slab_columns(dv):
    q, k, v = 0, LANES, 2 * LANES
    hi = v + dv
    lo = hi + LANES
    gate = lo + LANES
    return q, k, v, hi, lo, gate, gate + dv


def _store_head_slabs(slab_ref, q, k, v, log_decay, gate, dv):
    g2 = log_decay * LOG2E
    hi = g2.astype(BF16)
    lo = (g2 - hi.astype(F32)).astype(BF16)
    cq, ck, cv, chi, clo, cgate, _ = _slab_columns(dv)
    for h in range(slab_ref.shape[0]):
        dk_cols = slice(h * LANES, (h + 1) * LANES)
        dv_cols = slice(h * dv, (h + 1) * dv)
        slab_ref[h, :, cq:cq + LANES] = q[:, dk_cols].astype(BF16)
        slab_ref[h, :, ck:ck + LANES] = k[:, dk_cols].astype(BF16)
        slab_ref[h, :, cv:cv + dv] = v[:, dv_cols].astype(BF16)
        slab_ref[h, :, chi:chi + LANES] = hi[:, dk_cols]
        slab_ref[h, :, clo:clo + LANES] = lo[:, dk_cols]
        slab_ref[h, :, cgate:cgate + dv] = gate[:, dv_cols].astype(BF16)


def _memkv_kernel(mem_ref, gain_ref, w_ref, o_ref):
    h = _rms_normed(mem_ref[...], gain_ref[0]).astype(BF16)
    o_ref[0] = jnp.dot(h, w_ref[0], preferred_element_type=F32).astype(BF16)


def _memkv(mem2d, norm_mem, w_kv):
    rows = mem2d.shape[0]
    return pl.pallas_call(
        _memkv_kernel,
        grid=(DEPTH,),
        in_specs=[
            _resident((rows, D_MODEL)),
            pl.BlockSpec((1, 1, D_MODEL), lambda l: (l, 0, 0)),
            pl.BlockSpec((1, D_MODEL, 2 * XA_DIM), lambda l: (l, 0, 0)),
        ],
        out_specs=pl.BlockSpec((1, rows, 2 * XA_DIM), lambda l: (l, 0, 0)),
        out_shape=jax.ShapeDtypeStruct((DEPTH, rows, 2 * XA_DIM), BF16),
        compiler_params=_params(("arbitrary",)),
        name="memkv",
    )(mem2d, norm_mem.reshape(DEPTH, 1, D_MODEL), w_kv)


def _proj_out_specs(heads, dv):
    width = _slab_columns(dv)[-1]
    return [pl.BlockSpec((heads, ROW_TILE, width), lambda i: (0, i, 0)),
            pl.BlockSpec((ROW_TILE, XA_DIM), lambda i: (i, 0))]


def _proj_out_shapes(n, heads, dv):
    width = _slab_columns(dv)[-1]
    return [jax.ShapeDtypeStruct((heads, n, width), BF16), jax.ShapeDtypeStruct((n, XA_DIM), BF16)]


_GLA_QKV = 2 * GLA_HEADS * GLA_DK + GLA_HEADS * GLA_DV
_GLA_RX = GLA_HEADS * GLA_DV + XA_DIM


def _proj_gla_kernel(x_ref, gain_ref, wqkv_ref, wg_ref, wrx_ref, wg2_ref, bg_ref,
                     slab_ref, xq_ref):
    h = _rms_normed(x_ref[...], gain_ref[...]).astype(BF16)

    def mm(w_ref, a, b):
        return jnp.dot(h, w_ref[:, a:b], preferred_element_type=F32)

    q = mm(wqkv_ref, 0, 512) * (GLA_DK ** -0.5)
    k = mm(wqkv_ref, 512, 1024)
    v = mm(wqkv_ref, 1024, 2048)
    g_lr = mm(wg_ref, 0, GATE_PAD).astype(BF16)
    z = jnp.dot(g_lr, wg2_ref[...], preferred_element_type=F32) + bg_ref[...]
    log_alpha = (jnp.minimum(z, 0.0) - jnp.log(1.0 + jnp.exp(-jnp.abs(z)))) * (1.0 / GLA_TAU)
    gate = _silu(mm(wrx_ref, 0, 1024))
    _store_head_slabs(slab_ref, q, k, v, log_alpha, gate, GLA_DV)
    xq_ref[...] = mm(wrx_ref, 1024, _GLA_RX).astype(BF16)


def _proj_gla(x2d, gain, w_in, w_gate2, b_gate):
    n = x2d.shape[0]
    g0, g1 = _GLA_QKV, _GLA_QKV + GLA_GATE_RANK
    w_qkv = w_in[:, :g0].astype(BF16)
    w_g = jnp.pad(w_in[:, g0:g1], ((0, 0), (0, GATE_PAD - GLA_GATE_RANK))).astype(BF16)
    w_rx = w_in[:, g1:].astype(BF16)
    wg2 = jnp.pad(w_gate2, ((0, GATE_PAD - GLA_GATE_RANK), (0, 0))).astype(BF16)
    row = lambda width: pl.BlockSpec((ROW_TILE, width), lambda i: (i, 0))
    return pl.pallas_call(
        _proj_gla_kernel,
        grid=(n // ROW_TILE,),
        in_specs=[row(D_MODEL), _resident((1, D_MODEL)), _resident((D_MODEL, _GLA_QKV)),
                  _resident((D_MODEL, GATE_PAD)), _resident((D_MODEL, _GLA_RX)),
                  _resident((GATE_PAD, 512)), _resident((1, 512))],
        out_specs=_proj_out_specs(GLA_HEADS, GLA_DV),
        out_shape=_proj_out_shapes(n, GLA_HEADS, GLA_DV),
        compiler_params=_params(("parallel",)),
        name="proj_gla",
    )(x2d, gain.reshape(1, D_MODEL), w_qkv, w_g, w_rx, wg2, b_gate.reshape(1, 512))


def _proj_hgrn_kernel(x_ref, gain_ref, w_ref, lbraw_ref,
                      slab_ref, xq_ref, *, layer):
    h = _rms_normed(x_ref[...], gain_ref[...]).astype(BF16)

    def mm(a, b):
        return jnp.dot(h, w_ref[:, a:b], preferred_element_type=F32)

    raw = lbraw_ref[...]
    e = jnp.exp(raw - jnp.max(raw, axis=0, keepdims=True))
    p = e / jnp.sum(e, axis=0, keepdims=True)
    lb = jnp.sum(p[1:layer + 1, :], axis=0, keepdims=True)

    q = _silu(mm(0, 1024)) * (HGRN_DF ** -0.5)
    forget = lb + (1.0 - lb) * _sigmoid(mm(1024, 2048))
    v = mm(2048, 3072)
    gate = _silu(mm(3072, 4096))
    _store_head_slabs(slab_ref, q, 1.0 - forget, v, jnp.log(forget), gate, HGRN_DI)
    xq_ref[...] = mm(4096, 4608).astype(BF16)


def _proj_hgrn(x2d, gain, w_in, lower_bounds_raw, layer):
    n = x2d.shape[0]
    return pl.pallas_call(
        functools.partial(_proj_hgrn_kernel, layer=layer),
        grid=(n // ROW_TILE,),
        in_specs=[pl.BlockSpec((ROW_TILE, D_MODEL), lambda i: (i, 0)), _resident((1, D_MODEL)),
                  _resident((D_MODEL, 4608)), _resident((DEPTH, 1024))],
        out_specs=_proj_out_specs(HGRN_HEADS, HGRN_DI),
        out_shape=_proj_out_shapes(n, HGRN_HEADS, HGRN_DI),
        compiler_params=_params(("parallel",)),
        name="proj_hgrn",
    )(x2d, gain.reshape(1, D_MODEL), w_in.astype(BF16), lower_bounds_raw)


def _level_rows(b, g2, q, k, n):
    if n >= SUBLANES:
        split = lambda a: a.reshape(MIX_GROUP // (2 * n), 2, n, LANES)
        b4 = split(b)
        r = b4[:, 0, n - 1:n, :]
        lower = split(k)[:, 0] * jnp.exp2(r - b4[:, 0])
        upper = split(q)[:, 1] * jnp.exp2(b4[:, 1] - r)
        return jnp.stack([lower, upper], axis=1).reshape(MIX_GROUP, LANES)
    tile = lambda a: a.reshape(MIX_GROUP // SUBLANES, SUBLANES, LANES)
    sub = lax.broadcasted_iota(jnp.int32, (1, SUBLANES, LANES), 1)
    is_upper = (sub & n) != 0
    if n == 1:
        e = jnp.where(is_upper, tile(g2), 0.0)
    else:
        b3 = tile(b)
        r = b3[:, n - 1:n, :]
        for start in range(2 * n, SUBLANES, 2 * n):
            r = jnp.where(sub >= start, b3[:, start + n - 1:start + n, :], r)
        d = b3 - r
        e = jnp.where(is_upper, d, -d)
    x = jnp.where(is_upper, tile(q), tile(k)) * jnp.exp2(e)
    return x.reshape(MIX_GROUP, LANES)


def _mixer_kernel(slab_ref, gain_ref, o_ref, s_ref, sall_ref, *, dv):
    @pl.when(pl.program_id(2) == 0)
    def _():
        s_ref[...] = jnp.zeros_like(s_ref)

    def tri_masks(n_chunks):
        shape3 = (n_chunks, CHUNK, CHUNK)
        rows = lax.broadcasted_iota(jnp.int32, shape3, 1)
        cols = lax.broadcasted_iota(jnp.int32, shape3, 2)
        return rows >= cols, rows ^ cols

    cq, ck, cv, chi, clo, cgate, _ = _slab_columns(dv)
    causal_all, _ = tri_masks(N_CHUNKS)
    hi_lo = slab_ref[0, :, chi:chi + 2 * LANES]
    bb = lax.dot_general(causal_all.astype(BF16), hi_lo.reshape(N_CHUNKS, CHUNK, 2 * LANES),
                         _BNN, preferred_element_type=F32)
    b_all = (bb[:, :, :LANES] + bb[:, :, LANES:]).reshape(MIX_TILE, LANES)

    causal, sep = tri_masks(1)

    def chunks(a):
        return a.reshape(GROUP_CHUNKS, CHUNK, a.shape[-1])

    s = s_ref[...]
    for group in range(MIX_TILE // MIX_GROUP):
        lo = group * MIX_GROUP
        rs = pl.ds(lo, MIX_GROUP)
        sall = sall_ref.at[pl.ds(group * GROUP_CHUNKS, GROUP_CHUNKS)]
        b = b_all[lo:lo + MIX_GROUP, :]
        g2 = (slab_ref[0, rs, chi:chi + LANES].astype(F32)
              + slab_ref[0, rs, clo:clo + LANES].astype(F32))
        q = slab_ref[0, rs, cq:cq + LANES].astype(F32)
        k = slab_ref[0, rs, ck:ck + LANES].astype(F32)
        v = chunks(slab_ref[0, rs, cv:cv + dv])

        attn = jnp.broadcast_to(jnp.sum(chunks(q * k), axis=-1, keepdims=True),
                                (GROUP_CHUNKS, CHUNK, CHUNK))
        for n in LEVELS:
            x = chunks(_level_rows(b, g2, q, k, n).astype(BF16))
            p = lax.dot_general(x, x, _BNT, preferred_element_type=F32)
            attn = jnp.where(sep >= n, p, attn)
        attn = jnp.where(causal, attn, 0.0).astype(BF16)

        b3 = chunks(b)
        b_last = b3[:, CHUNK - 1:CHUNK, :]
        qd = (chunks(q) * jnp.exp2(b3)).astype(BF16)
        kd = (chunks(k) * jnp.exp2(b_last - b3)).astype(BF16)
        u = lax.dot_general(kd, v, _BTN, preferred_element_type=F32)
        decay = jnp.swapaxes(jnp.broadcast_to(jnp.exp2(b_last), (GROUP_CHUNKS, LANES, LANES)), 1, 2)
        if dv > LANES:
            decay = jnp.concatenate([decay] * (dv // LANES), axis=2)

        for c in range(GROUP_CHUNKS):
            sall[c] = s.astype(BF16)
            s = decay[c] * s + u[c]

        o = (lax.dot_general(attn, v, _BNN, preferred_element_type=F32)
             + lax.dot_general(qd, sall[...], _BNN, preferred_element_type=F32))
        gate = slab_ref[0, rs, cgate:cgate + dv].astype(F32)
        y = _rms_normed(o.reshape(MIX_GROUP, dv), gain_ref[...]) * gate
        o_ref[0, rs, :] = y.astype(o_ref.dtype)
    s_ref[...] = s


def _mixer(slab, out_gain, *, batch):
    heads, n, width = slab.shape
    dv = out_gain.shape[0]
    tiles = n // batch // MIX_TILE
    block = lambda cols: pl.BlockSpec((1, MIX_TILE, cols), lambda b, h, t: (h, b * tiles + t, 0))
    return pl.pallas_call(
        functools.partial(_mixer_kernel, dv=dv),
        grid=(batch, heads, tiles),
        in_specs=[block(width), _resident((1, dv))],
        out_specs=block(dv),
        out_shape=jax.ShapeDtypeStruct((heads, n, dv), BF16),
        scratch_shapes=[pltpu.VMEM((LANES, dv), F32), pltpu.VMEM((N_CHUNKS, LANES, dv), BF16)],
        compiler_params=_params(("parallel", "parallel", "arbitrary")),
        name=f"mixer_dv{dv}",
    )(slab, out_gain.reshape(1, dv))


def _attn_mlp_kernel(x_ref, ymix_ref, xq_ref, kv_ref, wout_ref, gain_ref, wup_ref, wdn_ref,
                     fgain_ref, o_ref, *, final):
    y_mix = jnp.concatenate([ymix_ref[h] for h in range(ymix_ref.shape[0])], axis=1)
    acc = x_ref[...] + jnp.dot(y_mix, wout_ref[0:D_MODEL, :], preferred_element_type=F32)
    xq = xq_ref[...]
    kv = kv_ref[0]
    heads = []
    for h in range(XA_HEADS):
        lo, hi = h * XA_DH, (h + 1) * XA_DH
        s = lax.dot_general(xq[:, lo:hi], kv[:, lo:hi], _NT, preferred_element_type=F32)
        s = s * (XA_DH ** -0.5)
        e = jnp.exp(s - jnp.max(s, axis=-1, keepdims=True)).astype(BF16)
        vh = kv[:, XA_DIM + lo:XA_DIM + hi]
        num_den = jnp.dot(e, jnp.concatenate([vh, jnp.ones_like(vh)], axis=1),
                          preferred_element_type=F32)
        heads.append((num_den[:, :XA_DH] / num_den[:, XA_DH:]).astype(BF16))
    y_mem = jnp.concatenate(heads, axis=1)
    x = acc + jnp.dot(y_mem, wout_ref[D_MODEL:, :], preferred_element_type=F32)

    h = _rms_normed(x, gain_ref[...]).astype(BF16)
    mlp = None
    for c in range(0, MLP_HIDDEN, MLP_HIDDEN_TILE):
        u = jnp.maximum(jnp.dot(h, wup_ref[:, c:c + MLP_HIDDEN_TILE], preferred_element_type=F32), 0.0)
        d = jnp.dot((u * u).astype(BF16), wdn_ref[c:c + MLP_HIDDEN_TILE, :],
                    preferred_element_type=F32)
        mlp = d if mlp is None else mlp + d
    out = x + mlp
    o_ref[...] = _rms_normed(out, fgain_ref[...]) if final else out


def _attn_mlp(x2d, y_mix, xq, kv, w_out, gain, w_up, w_down, final_gain, layer, *, batch, final):
    n = x2d.shape[0]
    tiles = n // batch // ATTN_TILE
    row = lambda width: pl.BlockSpec((ATTN_TILE, width), lambda i: (i, 0))
    heads, _, dv = y_mix.shape
    return pl.pallas_call(
        functools.partial(_attn_mlp_kernel, final=final),
        grid=(n // ATTN_TILE,),
        in_specs=[row(D_MODEL), pl.BlockSpec((heads, ATTN_TILE, dv), lambda i: (0, i, 0)),
                  row(XA_DIM),
                  pl.BlockSpec((1, MEM_LEN, 2 * XA_DIM), lambda i: (layer * batch + i // tiles, 0, 0)),
                  _resident((D_MODEL + XA_DIM, D_MODEL)), _resident((1, D_MODEL)),
                  _resident((D_MODEL, MLP_HIDDEN)), _resident((MLP_HIDDEN, D_MODEL)),
                  _resident((1, D_MODEL))],
        out_specs=row(D_MODEL),
        out_shape=jax.ShapeDtypeStruct((n, D_MODEL), F32),
        compiler_params=_params(("parallel",)),
        name="attn_mlp_final" if final else "attn_mlp",
    )(x2d, y_mix, xq, kv, w_out.astype(BF16), gain.reshape(1, D_MODEL), w_up.astype(BF16),
      w_down.astype(BF16), final_gain.reshape(1, D_MODEL))


def kernel(x, mem, norm_mix, norm_mem, w_kv, w_out, norm_mlp, w_up, w_down,
           gla_w_in, gla_w_gate2, gla_b_gate, gla_out_gain,
           hgrn_w_in, hgrn_lower_bounds, hgrn_out_gain, final_norm):
    batch, seq, _ = x.shape
    assert seq % MIX_TILE == 0 and seq % ROW_TILE == 0 and seq % ATTN_TILE == 0
    x2d = x.reshape(batch * seq, D_MODEL)
    kv = _memkv(mem.reshape(batch * MEM_LEN, D_MODEL), norm_mem, w_kv.astype(BF16))
    kv = kv.reshape(DEPTH * batch, MEM_LEN, 2 * XA_DIM)

    for i in range(DEPTH):
        j = i // 2
        if i % 2 == 0:
            slab, xq = _proj_gla(x2d, norm_mix[i], gla_w_in[j], gla_w_gate2[j], gla_b_gate[j])
            y_mix = _mixer(slab, gla_out_gain[j], batch=batch)
        else:
            slab, xq = _proj_hgrn(x2d, norm_mix[i], hgrn_w_in[j], hgrn_lower_bounds, i)
            y_mix = _mixer(slab, hgrn_out_gain[j], batch=batch)
        x2d = _attn_mlp(x2d, y_mix, xq, kv, w_out[i], norm_mlp[i], w_up[i], w_down[i], final_norm,
                        i, batch=batch, final=(i == DEPTH - 1))
    return x2d.reshape(batch, seq, D_MODEL)
```

```python
import functools

import jax
import jax.numpy as jnp
from jax import lax
from jax.experimental import pallas as pl
from jax.experimental.pallas import tpu as pltpu

F32 = jnp.float32
BF16 = jnp.bfloat16

D_MODEL = 1024
DEPTH = 4
MEM_LEN = 256
NORM_EPS = 1e-6
GLA_HEADS = 4
GLA_DK = 128
GLA_DV = 256
GLA_GATE_RANK = 16
GLA_TAU = 16.0
HGRN_HEADS = 8
HGRN_DF = 128
HGRN_DI = 128
XA_HEADS = 4
XA_DH = 128
XA_DIM = XA_HEADS * XA_DH
MLP_HIDDEN = 4 * D_MODEL

LANES = 128
SUBLANES = 8
V7X_VMEM_BYTES = 64 * 1024 * 1024
VMEM_RESERVE_BYTES = 8 * 1024 * 1024
VMEM_LIMIT_BYTES = V7X_VMEM_BYTES - VMEM_RESERVE_BYTES

CHUNK = 64
ROW_TILE = 1024
ATTN_TILE = 1024
MIX_TILE = 4096
MIX_GROUP = 512
N_CHUNKS = MIX_TILE // CHUNK
GROUP_CHUNKS = MIX_GROUP // CHUNK
LEVELS = (1, 2, 4, 8, 16, 32)
MLP_HIDDEN_TILE = 1024
GATE_PAD = LANES
LOG2E = 1.4426950408889634

_NT = (((1,), (1,)), ((), ()))
_BNT = (((2,), (2,)), ((0,), (0,)))
_BNN = (((2,), (1,)), ((0,), (0,)))
_BTN = (((1,), (1,)), ((0,), (0,)))


def _params(semantics):
    return pltpu.CompilerParams(dimension_semantics=semantics, vmem_limit_bytes=VMEM_LIMIT_BYTES)


def _resident(shape):
    zeros = (0,) * len(shape)
    return pl.BlockSpec(shape, lambda *_: zeros, pipeline_mode=pl.Buffered(1))


def _rms_normed(x, gain):
    ms = jnp.mean(x * x, axis=-1, keepdims=True)
    return x * lax.rsqrt(ms + NORM_EPS) * gain


def _sigmoid(x):
    return 1.0 / (1.0 + jnp.exp(-x))


def _silu(x):
    half = 0.5 * x
    return half * jnp.tanh(half) + half


def _slab_columns(dv):
    q, k, v = 0, LANES, 2 * LANES
    hi = v + dv
    lo = hi + LANES
    gate = lo + LANES
    return q, k, v, hi, lo, gate, gate + dv


def _store_head_slabs(slab_ref, q, k, v, log_decay, gate, dv):
    g2 = log_decay * LOG2E
    hi = g2.astype(BF16)
    lo = (g2 - hi.astype(F32)).astype(BF16)
    cq, ck, cv, chi, clo, cgate, _ = _slab_columns(dv)
    for h in range(slab_ref.shape[0]):
        dk_cols = slice(h * LANES, (h + 1) * LANES)
        dv_cols = slice(h * dv, (h + 1) * dv)
        slab_ref[h, :, cq:cq + LANES] = q[:, dk_cols].astype(BF16)
        slab_ref[h, :, ck:ck + LANES] = k[:, dk_cols].astype(BF16)
        slab_ref[h, :, cv:cv + dv] = v[:, dv_cols].astype(BF16)
        slab_ref[h, :, chi:chi + LANES] = hi[:, dk_cols]
        slab_ref[h, :, clo:clo + LANES] = lo[:, dk_cols]
        slab_ref[h, :, cgate:cgate + dv] = gate[:, dv_cols].astype(BF16)


def _memkv_kernel(mem_ref, gain_ref, w_ref, o_ref):
    h = _rms_normed(mem_ref[...], gain_ref[0]).astype(BF16)
    o_ref[0] = jnp.dot(h, w_ref[0], preferred_element_type=F32).astype(BF16)


def _memkv(mem2d, norm_mem, w_kv):
    rows = mem2d.shape[0]
    return pl.pallas_call(
        _memkv_kernel,
        grid=(DEPTH,),
        in_specs=[
            _resident((rows, D_MODEL)),
            pl.BlockSpec((1, 1, D_MODEL), lambda l: (l, 0, 0)),
            pl.BlockSpec((1, D_MODEL, 2 * XA_DIM), lambda l: (l, 0, 0)),
        ],
        out_specs=pl.BlockSpec((1, rows, 2 * XA_DIM), lambda l: (l, 0, 0)),
        out_shape=jax.ShapeDtypeStruct((DEPTH, rows, 2 * XA_DIM), BF16),
        compiler_params=_params(("arbitrary",)),
        name="memkv",
    )(mem2d, norm_mem.reshape(DEPTH, 1, D_MODEL), w_kv)


def _proj_out_specs(heads, dv):
    width = _slab_columns(dv)[-1]
    return [pl.BlockSpec((heads, ROW_TILE, width), lambda i: (0, i, 0)),
            pl.BlockSpec((ROW_TILE, XA_DIM), lambda i: (i, 0))]


def _proj_out_shapes(n, heads, dv):
    width = _slab_columns(dv)[-1]
    return [jax.ShapeDtypeStruct((heads, n, width), BF16), jax.ShapeDtypeStruct((n, XA_DIM), BF16)]


_GLA_QK = GLA_HEADS * GLA_DK
_GLA_QKV = 2 * _GLA_QK + D_MODEL
_GLA_RX = D_MODEL + XA_DIM


def _proj_gla_kernel(x_ref, gain_ref, wqkv_ref, wg_ref, wrx_ref, wg2_ref, bg_ref,
                     slab_ref, xq_ref):
    h = _rms_normed(x_ref[...], gain_ref[...]).astype(BF16)

    def mm(w_ref, a, b):
        return jnp.dot(h, w_ref[:, a:b], preferred_element_type=F32)

    q = mm(wqkv_ref, 0, _GLA_QK) * (GLA_DK ** -0.5)
    k = mm(wqkv_ref, _GLA_QK, 2 * _GLA_QK)
    v = mm(wqkv_ref, 2 * _GLA_QK, _GLA_QKV)
    g_lr = mm(wg_ref, 0, GATE_PAD).astype(BF16)
    z = jnp.dot(g_lr, wg2_ref[...], preferred_element_type=F32) + bg_ref[...]
    log_alpha = (jnp.minimum(z, 0.0) - jnp.log(1.0 + jnp.exp(-jnp.abs(z)))) * (1.0 / GLA_TAU)
    gate = _silu(mm(wrx_ref, 0, D_MODEL))
    _store_head_slabs(slab_ref, q, k, v, log_alpha, gate, GLA_DV)
    xq_ref[...] = mm(wrx_ref, D_MODEL, _GLA_RX).astype(BF16)


def _proj_gla(x2d, gain, w_in, w_gate2, b_gate):
    n = x2d.shape[0]
    g0, g1 = _GLA_QKV, _GLA_QKV + GLA_GATE_RANK
    w_qkv = w_in[:, :g0].astype(BF16)
    w_g = jnp.pad(w_in[:, g0:g1], ((0, 0), (0, GATE_PAD - GLA_GATE_RANK))).astype(BF16)
    w_rx = w_in[:, g1:].astype(BF16)
    wg2 = jnp.pad(w_gate2, ((0, GATE_PAD - GLA_GATE_RANK), (0, 0))).astype(BF16)
    row = lambda width: pl.BlockSpec((ROW_TILE, width), lambda i: (i, 0))
    return pl.pallas_call(
        _proj_gla_kernel,
        grid=(n // ROW_TILE,),
        in_specs=[row(D_MODEL), _resident((1, D_MODEL)), _resident((D_MODEL, _GLA_QKV)),
                  _resident((D_MODEL, GATE_PAD)), _resident((D_MODEL, _GLA_RX)),
                  _resident((GATE_PAD, _GLA_QK)), _resident((1, _GLA_QK))],
        out_specs=_proj_out_specs(GLA_HEADS, GLA_DV),
        out_shape=_proj_out_shapes(n, GLA_HEADS, GLA_DV),
        compiler_params=_params(("parallel",)),
        name="proj_gla",
    )(x2d, gain.reshape(1, D_MODEL), w_qkv, w_g, w_rx, wg2, b_gate.reshape(1, _GLA_QK))


_HGRN_COLS = 4 * D_MODEL + XA_DIM


def _proj_hgrn_kernel(x_ref, gain_ref, w_ref, lbraw_ref, slab_ref, xq_ref, *, layer):
    h = _rms_normed(x_ref[...], gain_ref[...]).astype(BF16)

    def mm(a, b):
        return jnp.dot(h, w_ref[:, a:b], preferred_element_type=F32)

    raw = lbraw_ref[...]
    e = jnp.exp(raw - jnp.max(raw, axis=0, keepdims=True))
    p = e / jnp.sum(e, axis=0, keepdims=True)
    lb = jnp.sum(p[1:layer + 1, :], axis=0, keepdims=True)

    d = D_MODEL
    q = _silu(mm(0, d)) * (HGRN_DF ** -0.5)
    forget = lb + (1.0 - lb) * _sigmoid(mm(d, 2 * d))
    v = mm(2 * d, 3 * d)
    gate = _silu(mm(3 * d, 4 * d))
    _store_head_slabs(slab_ref, q, 1.0 - forget, v, jnp.log(forget), gate, HGRN_DI)
    xq_ref[...] = mm(4 * d, _HGRN_COLS).astype(BF16)


def _proj_hgrn(x2d, gain, w_in, lower_bounds_raw, layer):
    n = x2d.shape[0]
    return pl.pallas_call(
        functools.partial(_proj_hgrn_kernel, layer=layer),
        grid=(n // ROW_TILE,),
        in_specs=[pl.BlockSpec((ROW_TILE, D_MODEL), lambda i: (i, 0)), _resident((1, D_MODEL)),
                  _resident((D_MODEL, _HGRN_COLS)), _resident((DEPTH, D_MODEL))],
        out_specs=_proj_out_specs(HGRN_HEADS, HGRN_DI),
        out_shape=_proj_out_shapes(n, HGRN_HEADS, HGRN_DI),
        compiler_params=_params(("parallel",)),
        name="proj_hgrn",
    )(x2d, gain.reshape(1, D_MODEL), w_in.astype(BF16), lower_bounds_raw)


def _level_rows(b, g2, q, k, n):
    if n >= SUBLANES:
        split = lambda a: a.reshape(MIX_GROUP // (2 * n), 2, n, LANES)
        b4 = split(b)
        r = b4[:, 0, n - 1:n, :]
        lower = split(k)[:, 0] * jnp.exp2(r - b4[:, 0])
        upper = split(q)[:, 1] * jnp.exp2(b4[:, 1] - r)
        return jnp.stack([lower, upper], axis=1).reshape(MIX_GROUP, LANES)
    tile = lambda a: a.reshape(MIX_GROUP // SUBLANES, SUBLANES, LANES)
    sub = lax.broadcasted_iota(jnp.int32, (1, SUBLANES, LANES), 1)
    is_upper = (sub & n) != 0
    if n == 1:
        e = jnp.where(is_upper, tile(g2), 0.0)
    else:
        b3 = tile(b)
        r = b3[:, n - 1:n, :]
        for start in range(2 * n, SUBLANES, 2 * n):
            r = jnp.where(sub >= start, b3[:, start + n - 1:start + n, :], r)
        d = b3 - r
        e = jnp.where(is_upper, d, -d)
    x = jnp.where(is_upper, tile(q), tile(k)) * jnp.exp2(e)
    return x.reshape(MIX_GROUP, LANES)


def _mixer_kernel(slab_ref, gain_ref, o_ref, s_ref, sall_ref, *, dv):
    @pl.when(pl.program_id(2) == 0)
    def _():
        s_ref[...] = jnp.zeros_like(s_ref)

    def tri_masks(n_chunks):
        shape3 = (n_chunks, CHUNK, CHUNK)
        rows = lax.broadcasted_iota(jnp.int32, shape3, 1)
        cols = lax.broadcasted_iota(jnp.int32, shape3, 2)
        return rows >= cols, rows ^ cols

    cq, ck, cv, chi, clo, cgate, _ = _slab_columns(dv)
    causal_all, _ = tri_masks(N_CHUNKS)
    hi_lo = slab_ref[0, :, chi:chi + 2 * LANES]
    bb = lax.dot_general(causal_all.astype(BF16), hi_lo.reshape(N_CHUNKS, CHUNK, 2 * LANES),
                         _BNN, preferred_element_type=F32)
    b_all = (bb[:, :, :LANES] + bb[:, :, LANES:]).reshape(MIX_TILE, LANES)

    causal, sep = tri_masks(1)

    def chunks(a):
        return a.reshape(GROUP_CHUNKS, CHUNK, a.shape[-1])

    s = s_ref[...]
    for group in range(MIX_TILE // MIX_GROUP):
        lo = group * MIX_GROUP
        rs = pl.ds(lo, MIX_GROUP)
        sall = sall_ref.at[pl.ds(group * GROUP_CHUNKS, GROUP_CHUNKS)]
        b = b_all[lo:lo + MIX_GROUP, :]
        g2 = (slab_ref[0, rs, chi:chi + LANES].astype(F32)
              + slab_ref[0, rs, clo:clo + LANES].astype(F32))
        q = slab_ref[0, rs, cq:cq + LANES].astype(F32)
        k = slab_ref[0, rs, ck:ck + LANES].astype(F32)
        v = chunks(slab_ref[0, rs, cv:cv + dv])

        attn = jnp.broadcast_to(jnp.sum(chunks(q * k), axis=-1, keepdims=True),
                                (GROUP_CHUNKS, CHUNK, CHUNK))
        for n in LEVELS:
            x = chunks(_level_rows(b, g2, q, k, n).astype(BF16))
            p = lax.dot_general(x, x, _BNT, preferred_element_type=F32)
            attn = jnp.where(sep >= n, p, attn)
        attn = jnp.where(causal, attn, 0.0).astype(BF16)

        b3 = chunks(b)
        b_last = b3[:, CHUNK - 1:CHUNK, :]
        qd = (chunks(q) * jnp.exp2(b3)).astype(BF16)
        kd = (chunks(k) * jnp.exp2(b_last - b3)).astype(BF16)
        u = lax.dot_general(kd, v, _BTN, preferred_element_type=F32)
        decay = jnp.swapaxes(jnp.broadcast_to(jnp.exp2(b_last), (GROUP_CHUNKS, LANES, LANES)), 1, 2)
        if dv > LANES:
            decay = jnp.concatenate([decay] * (dv // LANES), axis=2)

        for c in range(GROUP_CHUNKS):
            sall[c] = s.astype(BF16)
            s = decay[c] * s + u[c]

        o = (lax.dot_general(attn, v, _BNN, preferred_element_type=F32)
             + lax.dot_general(qd, sall[...], _BNN, preferred_element_type=F32))
        gate = slab_ref[0, rs, cgate:cgate + dv].astype(F32)
        y = _rms_normed(o.reshape(MIX_GROUP, dv), gain_ref[...]) * gate
        o_ref[0, rs, :] = y.astype(o_ref.dtype)
    s_ref[...] = s


def _mixer(slab, out_gain, *, batch):
    heads, n, width = slab.shape
    dv = out_gain.shape[0]
    tiles = n // batch // MIX_TILE
    block = lambda cols: pl.BlockSpec((1, MIX_TILE, cols), lambda b, h, t: (h, b * tiles + t, 0))
    return pl.pallas_call(
        functools.partial(_mixer_kernel, dv=dv),
        grid=(batch, heads, tiles),
        in_specs=[block(width), _resident((1, dv))],
        out_specs=block(dv),
        out_shape=jax.ShapeDtypeStruct((heads, n, dv), BF16),
        scratch_shapes=[pltpu.VMEM((LANES, dv), F32), pltpu.VMEM((N_CHUNKS, LANES, dv), BF16)],
        compiler_params=_params(("parallel", "parallel", "arbitrary")),
        name=f"mixer_dv{dv}",
    )(slab, out_gain.reshape(1, dv))


def _attn_mlp_kernel(x_ref, ymix_ref, xq_ref, kv_ref, wout_ref, gain_ref, wup_ref, wdn_ref,
                     fgain_ref, o_ref, *, final):
    y_mix = jnp.concatenate([ymix_ref[h] for h in range(ymix_ref.shape[0])], axis=1)
    acc = x_ref[...] + jnp.dot(y_mix, wout_ref[0:D_MODEL, :], preferred_element_type=F32)
    xq = xq_ref[...]
    kv = kv_ref[0]
    heads = []
    for h in range(XA_HEADS):
        lo, hi = h * XA_DH, (h + 1) * XA_DH
        s = lax.dot_general(xq[:, lo:hi], kv[:, lo:hi], _NT, preferred_element_type=F32)
        s = s * (XA_DH ** -0.5)
        e = jnp.exp(s - jnp.max(s, axis=-1, keepdims=True)).astype(BF16)
        vh = kv[:, XA_DIM + lo:XA_DIM + hi]
        num_den = jnp.dot(e, jnp.concatenate([vh, jnp.ones_like(vh)], axis=1),
                          preferred_element_type=F32)
        heads.append((num_den[:, :XA_DH] / num_den[:, XA_DH:]).astype(BF16))
    y_mem = jnp.concatenate(heads, axis=1)
    x = acc + jnp.dot(y_mem, wout_ref[D_MODEL:, :], preferred_element_type=F32)

    h = _rms_normed(x, gain_ref[...]).astype(BF16)
    mlp = None
    for c in range(0, MLP_HIDDEN, MLP_HIDDEN_TILE):
        u = jnp.maximum(jnp.dot(h, wup_ref[:, c:c + MLP_HIDDEN_TILE], preferred_element_type=F32), 0.0)
        d = jnp.dot((u * u).astype(BF16), wdn_ref[c:c + MLP_HIDDEN_TILE, :],
                    preferred_element_type=F32)
        mlp = d if mlp is None else mlp + d
    out = x + mlp
    o_ref[...] = _rms_normed(out, fgain_ref[...]) if final else out


def _attn_mlp(x2d, y_mix, xq, kv, w_out, gain, w_up, w_down, final_gain, layer, *, batch, final):
    n = x2d.shape[0]
    tiles = n // batch // ATTN_TILE
    row = lambda width: pl.BlockSpec((ATTN_TILE, width), lambda i: (i, 0))
    heads, _, dv = y_mix.shape
    return pl.pallas_call(
        functools.partial(_attn_mlp_kernel, final=final),
        grid=(n // ATTN_TILE,),
        in_specs=[row(D_MODEL), pl.BlockSpec((heads, ATTN_TILE, dv), lambda i: (0, i, 0)),
                  row(XA_DIM),
                  pl.BlockSpec((1, MEM_LEN, 2 * XA_DIM), lambda i: (layer * batch + i // tiles, 0, 0)),
                  _resident((D_MODEL + XA_DIM, D_MODEL)), _resident((1, D_MODEL)),
                  _resident((D_MODEL, MLP_HIDDEN)), _resident((MLP_HIDDEN, D_MODEL)),
                  _resident((1, D_MODEL))],
        out_specs=row(D_MODEL),
        out_shape=jax.ShapeDtypeStruct((n, D_MODEL), F32),
        compiler_params=_params(("parallel",)),
        name="attn_mlp_final" if final else "attn_mlp",
    )(x2d, y_mix, xq, kv, w_out.astype(BF16), gain.reshape(1, D_MODEL), w_up.astype(BF16),
      w_down.astype(BF16), final_gain.reshape(1, D_MODEL))


def kernel(x, mem, norm_mix, norm_mem, w_kv, w_out, norm_mlp, w_up, w_down,
           gla_w_in, gla_w_gate2, gla_b_gate, gla_out_gain,
           hgrn_w_in, hgrn_lower_bounds, hgrn_out_gain, final_norm):
    batch, seq, _ = x.shape
    assert seq % MIX_TILE == 0 and seq % ROW_TILE == 0 and seq % ATTN_TILE == 0
    x2d = x.reshape(batch * seq, D_MODEL)
    kv = _memkv(mem.reshape(batch * MEM_LEN, D_MODEL), norm_mem, w_kv.astype(BF16))
    kv = kv.reshape(DEPTH * batch, MEM_LEN, 2 * XA_DIM)

    for i in range(DEPTH):
        j = i // 2
        if i % 2 == 0:
            slab, xq = _proj_gla(x2d, norm_mix[i], gla_w_in[j], gla_w_gate2[j], gla_b_gate[j])
            y_mix = _mixer(slab, gla_out_gain[j], batch=batch)
        else:
            slab, xq = _proj_hgrn(x2d, norm_mix[i], hgrn_w_in[j], hgrn_lower_bounds, i)
            y_mix = _mixer(slab, hgrn_out_gain[j], batch=batch)
        x2d = _attn_mlp(x2d, y_mix, xq, kv, w_out[i], norm_mlp[i], w_up[i], w_down[i], final_norm,
                        i, batch=batch, final=(i == DEPTH - 1))
    return x2d.reshape(batch, seq, D_MODEL)
```

```python
import functools

import jax
import jax.numpy as jnp
from jax import lax
from jax.experimental import pallas as pl
from jax.experimental.pallas import tpu as pltpu

F32 = jnp.float32
BF16 = jnp.bfloat16

D_MODEL = 1024
DEPTH = 4
MEM_LEN = 256
NORM_EPS = 1e-6
GLA_HEADS = 4
GLA_DK = 128
GLA_DV = 256
GLA_GATE_RANK = 16
GLA_TAU = 16.0
HGRN_HEADS = 8
HGRN_DF = 128
HGRN_DI = 128
XA_HEADS = 4
XA_DH = 128
XA_DIM = XA_HEADS * XA_DH
MLP_HIDDEN = 4 * D_MODEL

LANES = 128
SUBLANES = 8
V7X_VMEM_BYTES = 64 * 1024 * 1024
VMEM_RESERVE_BYTES = 8 * 1024 * 1024
VMEM_LIMIT_BYTES = V7X_VMEM_BYTES - VMEM_RESERVE_BYTES

CHUNK = 64
ROW_TILE = 1024
ATTN_TILE = 1024
MIX_TILE = 4096
MIX_GROUP = 512
N_CHUNKS = MIX_TILE // CHUNK
GROUP_CHUNKS = MIX_GROUP // CHUNK
LEVELS = (1, 2, 4, 8, 16, 32)
MLP_HIDDEN_TILE = 1024
GATE_PAD = LANES
LOG2E = 1.4426950408889634

_NT = (((1,), (1,)), ((), ()))
_BNT = (((2,), (2,)), ((0,), (0,)))
_BNN = (((2,), (1,)), ((0,), (0,)))
_BTN = (((1,), (1,)), ((0,), (0,)))


def _params(semantics):
    return pltpu.CompilerParams(dimension_semantics=semantics, vmem_limit_bytes=VMEM_LIMIT_BYTES)


def _resident(shape):
    zeros = (0,) * len(shape)
    return pl.BlockSpec(shape, lambda *_: zeros, pipeline_mode=pl.Buffered(1))


def _rms_normed(x, gain):
    ms = jnp.mean(x * x, axis=-1, keepdims=True)
    return x * lax.rsqrt(ms + NORM_EPS) * gain


def _sigmoid(x):
    return 1.0 / (1.0 + jnp.exp(-x))


def _silu(x):
    half = 0.5 * x
    return half * jnp.tanh(half) + half


def _slab_columns(dv):
    q, k, v = 0, LANES, 2 * LANES
    hi = v + dv
    lo = hi + LANES
    gate = lo + LANES
    return q, k, v, hi, lo, gate, gate + dv


def _store_head_slab(slab_ref, head, q, k, v, log_decay, gate):
    dv = v.shape[1]
    g2 = log_decay * LOG2E
    hi = g2.astype(BF16)
    lo = (g2 - hi.astype(F32)).astype(BF16)
    cq, ck, cv, chi, clo, cgate, _ = _slab_columns(dv)
    slab_ref[head, :, cq:cq + LANES] = q.astype(BF16)
    slab_ref[head, :, ck:ck + LANES] = k.astype(BF16)
    slab_ref[head, :, cv:cv + dv] = v.astype(BF16)
    slab_ref[head, :, chi:chi + LANES] = hi
    slab_ref[head, :, clo:clo + LANES] = lo
    slab_ref[head, :, cgate:cgate + dv] = gate.astype(BF16)


def _memkv_kernel(mem_ref, gain_ref, w_ref, o_ref):
    h = _rms_normed(mem_ref[...], gain_ref[0]).astype(BF16)
    o_ref[0] = jnp.dot(h, w_ref[0], preferred_element_type=F32).astype(BF16)


def _memkv(mem2d, norm_mem, w_kv):
    rows = mem2d.shape[0]
    return pl.pallas_call(
        _memkv_kernel,
        grid=(DEPTH,),
        in_specs=[
            _resident((rows, D_MODEL)),
            pl.BlockSpec((1, 1, D_MODEL), lambda l: (l, 0, 0)),
            pl.BlockSpec((1, D_MODEL, 2 * XA_DIM), lambda l: (l, 0, 0)),
        ],
        out_specs=pl.BlockSpec((1, rows, 2 * XA_DIM), lambda l: (l, 0, 0)),
        out_shape=jax.ShapeDtypeStruct((DEPTH, rows, 2 * XA_DIM), BF16),
        compiler_params=_params(("arbitrary",)),
        name="memkv",
    )(mem2d, norm_mem.reshape(DEPTH, 1, D_MODEL), w_kv)


def _proj_out_specs(heads, dv):
    width = _slab_columns(dv)[-1]
    return [pl.BlockSpec((heads, ROW_TILE, width), lambda i: (0, i, 0)),
            pl.BlockSpec((ROW_TILE, XA_DIM), lambda i: (i, 0))]


def _proj_out_shapes(n, heads, dv):
    width = _slab_columns(dv)[-1]
    return [jax.ShapeDtypeStruct((heads, n, width), BF16), jax.ShapeDtypeStruct((n, XA_DIM), BF16)]


_GLA_QK = GLA_HEADS * GLA_DK
_GLA_HEAD_COLS = 2 * GLA_DK + 2 * GLA_DV
_HGRN_HEAD_COLS = 4 * LANES


def _proj_gla_kernel(x_ref, gain_ref, w_ref, wg_ref, wxq_ref, wg2_ref, bg_ref, slab_ref, xq_ref):
    h = _rms_normed(x_ref[...], gain_ref[...]).astype(BF16)
    g_lr = jnp.dot(h, wg_ref[...], preferred_element_type=F32).astype(BF16)
    for head in range(GLA_HEADS):
        lanes = slice(head * LANES, (head + 1) * LANES)
        cols = jnp.dot(h, w_ref[head], preferred_element_type=F32)
        z = jnp.dot(g_lr, wg2_ref[:, lanes], preferred_element_type=F32) + bg_ref[:, lanes]
        log_alpha = (jnp.minimum(z, 0.0) - jnp.log(1.0 + jnp.exp(-jnp.abs(z)))) * (1.0 / GLA_TAU)
        q = cols[:, 0:GLA_DK] * (GLA_DK ** -0.5)
        k = cols[:, GLA_DK:2 * GLA_DK]
        v = cols[:, 2 * GLA_DK:2 * GLA_DK + GLA_DV]
        gate = _silu(cols[:, 2 * GLA_DK + GLA_DV:])
        _store_head_slab(slab_ref, head, q, k, v, log_alpha, gate)
    xq_ref[...] = jnp.dot(h, wxq_ref[...], preferred_element_type=F32).astype(BF16)


def _proj_gla(x2d, gain, w_in, w_gate2, b_gate):
    n = x2d.shape[0]
    per_head = lambda a, width: a.reshape(D_MODEL, GLA_HEADS, width)
    g0 = 2 * _GLA_QK + D_MODEL
    r0 = g0 + GLA_GATE_RANK
    w_heads = jnp.concatenate(
        [per_head(w_in[:, :_GLA_QK], GLA_DK), per_head(w_in[:, _GLA_QK:2 * _GLA_QK], GLA_DK),
         per_head(w_in[:, 2 * _GLA_QK:g0], GLA_DV), per_head(w_in[:, r0:r0 + D_MODEL], GLA_DV)],
        axis=2).transpose(1, 0, 2).astype(BF16)
    w_g = jnp.pad(w_in[:, g0:r0], ((0, 0), (0, GATE_PAD - GLA_GATE_RANK))).astype(BF16)
    w_xq = w_in[:, r0 + D_MODEL:].astype(BF16)
    wg2 = jnp.pad(w_gate2, ((0, GATE_PAD - GLA_GATE_RANK), (0, 0))).astype(BF16)
    return pl.pallas_call(
        _proj_gla_kernel,
        grid=(n // ROW_TILE,),
        in_specs=[pl.BlockSpec((ROW_TILE, D_MODEL), lambda i: (i, 0)), _resident((1, D_MODEL)),
                  _resident((GLA_HEADS, D_MODEL, _GLA_HEAD_COLS)), _resident((D_MODEL, GATE_PAD)),
                  _resident((D_MODEL, XA_DIM)), _resident((GATE_PAD, _GLA_QK)),
                  _resident((1, _GLA_QK))],
        out_specs=_proj_out_specs(GLA_HEADS, GLA_DV),
        out_shape=_proj_out_shapes(n, GLA_HEADS, GLA_DV),
        compiler_params=_params(("parallel",)),
        name="proj_gla",
    )(x2d, gain.reshape(1, D_MODEL), w_heads, w_g, w_xq, wg2, b_gate.reshape(1, _GLA_QK))


def _proj_hgrn_kernel(x_ref, gain_ref, w_ref, wxq_ref, lbraw_ref, slab_ref, xq_ref, *, layer):
    h = _rms_normed(x_ref[...], gain_ref[...]).astype(BF16)

    raw = lbraw_ref[...]
    e = jnp.exp(raw - jnp.max(raw, axis=0, keepdims=True))
    p = e / jnp.sum(e, axis=0, keepdims=True)
    lb = jnp.sum(p[1:layer + 1, :], axis=0, keepdims=True)

    for head in range(HGRN_HEADS):
        lanes = slice(head * LANES, (head + 1) * LANES)
        cols = jnp.dot(h, w_ref[head], preferred_element_type=F32)
        q = _silu(cols[:, 0:LANES]) * (HGRN_DF ** -0.5)
        forget = lb[:, lanes] + (1.0 - lb[:, lanes]) * _sigmoid(cols[:, LANES:2 * LANES])
        v = cols[:, 2 * LANES:3 * LANES]
        gate = _silu(cols[:, 3 * LANES:])
        _store_head_slab(slab_ref, head, q, 1.0 - forget, v, jnp.log(forget), gate)
    xq_ref[...] = jnp.dot(h, wxq_ref[...], preferred_element_type=F32).astype(BF16)


def _proj_hgrn(x2d, gain, w_in, lower_bounds_raw, layer):
    n = x2d.shape[0]
    groups = 4 * D_MODEL
    w_heads = (w_in[:, :groups].reshape(D_MODEL, 4, HGRN_HEADS, LANES)
               .transpose(2, 0, 1, 3).reshape(HGRN_HEADS, D_MODEL, _HGRN_HEAD_COLS).astype(BF16))
    w_xq = w_in[:, groups:].astype(BF16)
    return pl.pallas_call(
        functools.partial(_proj_hgrn_kernel, layer=layer),
        grid=(n // ROW_TILE,),
        in_specs=[pl.BlockSpec((ROW_TILE, D_MODEL), lambda i: (i, 0)), _resident((1, D_MODEL)),
                  _resident((HGRN_HEADS, D_MODEL, _HGRN_HEAD_COLS)), _resident((D_MODEL, XA_DIM)),
                  _resident((DEPTH, D_MODEL))],
        out_specs=_proj_out_specs(HGRN_HEADS, HGRN_DI),
        out_shape=_proj_out_shapes(n, HGRN_HEADS, HGRN_DI),
        compiler_params=_params(("parallel",)),
        name="proj_hgrn",
    )(x2d, gain.reshape(1, D_MODEL), w_heads, w_xq, lower_bounds_raw)


def _level_rows(b, g2, q, k, n):
    if n >= SUBLANES:
        split = lambda a: a.reshape(MIX_GROUP // (2 * n), 2, n, LANES)
        b4 = split(b)
        r = b4[:, 0, n - 1:n, :]
        lower = split(k)[:, 0] * jnp.exp2(r - b4[:, 0])
        upper = split(q)[:, 1] * jnp.exp2(b4[:, 1] - r)
        return jnp.stack([lower, upper], axis=1).reshape(MIX_GROUP, LANES)
    tile = lambda a: a.reshape(MIX_GROUP // SUBLANES, SUBLANES, LANES)
    sub = lax.broadcasted_iota(jnp.int32, (1, SUBLANES, LANES), 1)
    is_upper = (sub & n) != 0
    if n == 1:
        e = jnp.where(is_upper, tile(g2), 0.0)
    else:
        b3 = tile(b)
        r = b3[:, n - 1:n, :]
        for start in range(2 * n, SUBLANES, 2 * n):
            r = jnp.where(sub >= start, b3[:, start + n - 1:start + n, :], r)
        d = b3 - r
        e = jnp.where(is_upper, d, -d)
    x = jnp.where(is_upper, tile(q), tile(k)) * jnp.exp2(e)
    return x.reshape(MIX_GROUP, LANES)


def _mixer_kernel(slab_ref, gain_ref, o_ref, s_ref, sall_ref, *, dv):
    @pl.when(pl.program_id(2) == 0)
    def _():
        s_ref[...] = jnp.zeros_like(s_ref)

    def tri_masks(n_chunks):
        shape3 = (n_chunks, CHUNK, CHUNK)
        rows = lax.broadcasted_iota(jnp.int32, shape3, 1)
        cols = lax.broadcasted_iota(jnp.int32, shape3, 2)
        return rows >= cols, rows ^ cols

    cq, ck, cv, chi, clo, cgate, _ = _slab_columns(dv)
    causal_all, _ = tri_masks(N_CHUNKS)
    hi_lo = slab_ref[0, :, chi:chi + 2 * LANES]
    bb = lax.dot_general(causal_all.astype(BF16), hi_lo.reshape(N_CHUNKS, CHUNK, 2 * LANES),
                         _BNN, preferred_element_type=F32)
    b_all = (bb[:, :, :LANES] + bb[:, :, LANES:]).reshape(MIX_TILE, LANES)

    causal, sep = tri_masks(1)

    def chunks(a):
        return a.reshape(GROUP_CHUNKS, CHUNK, a.shape[-1])

    s = s_ref[...]
    for group in range(MIX_TILE // MIX_GROUP):
        lo = group * MIX_GROUP
        rs = pl.ds(lo, MIX_GROUP)
        sall = sall_ref.at[pl.ds(group * GROUP_CHUNKS, GROUP_CHUNKS)]
        b = b_all[lo:lo + MIX_GROUP, :]
        g2 = (slab_ref[0, rs, chi:chi + LANES].astype(F32)
              + slab_ref[0, rs, clo:clo + LANES].astype(F32))
        q = slab_ref[0, rs, cq:cq + LANES].astype(F32)
        k = slab_ref[0, rs, ck:ck + LANES].astype(F32)
        v = chunks(slab_ref[0, rs, cv:cv + dv])

        attn = jnp.broadcast_to(jnp.sum(chunks(q * k), axis=-1, keepdims=True),
                                (GROUP_CHUNKS, CHUNK, CHUNK))
        for n in LEVELS:
            x = chunks(_level_rows(b, g2, q, k, n).astype(BF16))
            p = lax.dot_general(x, x, _BNT, preferred_element_type=F32)
            attn = jnp.where(sep >= n, p, attn)
        attn = jnp.where(causal, attn, 0.0).astype(BF16)

        b3 = chunks(b)
        b_last = b3[:, CHUNK - 1:CHUNK, :]
        qd = (chunks(q) * jnp.exp2(b3)).astype(BF16)
        kd = (chunks(k) * jnp.exp2(b_last - b3)).astype(BF16)
        u = lax.dot_general(kd, v, _BTN, preferred_element_type=F32)
        decay = jnp.swapaxes(jnp.broadcast_to(jnp.exp2(b_last), (GROUP_CHUNKS, LANES, LANES)), 1, 2)
        if dv > LANES:
            decay = jnp.concatenate([decay] * (dv // LANES), axis=2)

        for c in range(GROUP_CHUNKS):
            sall[c] = s.astype(BF16)
            s = decay[c] * s + u[c]

        o = (lax.dot_general(attn, v, _BNN, preferred_element_type=F32)
             + lax.dot_general(qd, sall[...], _BNN, preferred_element_type=F32))
        gate = slab_ref[0, rs, cgate:cgate + dv].astype(F32)
        y = _rms_normed(o.reshape(MIX_GROUP, dv), gain_ref[...]) * gate
        o_ref[0, rs, :] = y.astype(o_ref.dtype)
    s_ref[...] = s


def _mixer(slab, out_gain, *, batch):
    heads, n, width = slab.shape
    dv = out_gain.shape[0]
    tiles = n // batch // MIX_TILE
    block = lambda cols: pl.BlockSpec((1, MIX_TILE, cols), lambda b, h, t: (h, b * tiles + t, 0))
    return pl.pallas_call(
        functools.partial(_mixer_kernel, dv=dv),
        grid=(batch, heads, tiles),
        in_specs=[block(width), _resident((1, dv))],
        out_specs=block(dv),
        out_shape=jax.ShapeDtypeStruct((heads, n, dv), BF16),
        scratch_shapes=[pltpu.VMEM((LANES, dv), F32), pltpu.VMEM((N_CHUNKS, LANES, dv), BF16)],
        compiler_params=_params(("parallel", "parallel", "arbitrary")),
        name=f"mixer_dv{dv}",
    )(slab, out_gain.reshape(1, dv))


def _attn_mlp_kernel(x_ref, ymix_ref, xq_ref, kv_ref, wout_ref, gain_ref, wup_ref, wdn_ref,
                     fgain_ref, o_ref, *, final):
    y_mix = jnp.concatenate([ymix_ref[h] for h in range(ymix_ref.shape[0])], axis=1)
    acc = x_ref[...] + jnp.dot(y_mix, wout_ref[0:D_MODEL, :], preferred_element_type=F32)
    xq = xq_ref[...]
    kv = kv_ref[0]
    heads = []
    for h in range(XA_HEADS):
        lo, hi = h * XA_DH, (h + 1) * XA_DH
        s = lax.dot_general(xq[:, lo:hi], kv[:, lo:hi], _NT, preferred_element_type=F32)
        s = s * (XA_DH ** -0.5)
        e = jnp.exp(s - jnp.max(s, axis=-1, keepdims=True)).astype(BF16)
        vh = kv[:, XA_DIM + lo:XA_DIM + hi]
        num_den = jnp.dot(e, jnp.concatenate([vh, jnp.ones_like(vh)], axis=1),
                          preferred_element_type=F32)
        heads.append((num_den[:, :XA_DH] / num_den[:, XA_DH:]).astype(BF16))
    y_mem = jnp.concatenate(heads, axis=1)
    x = acc + jnp.dot(y_mem, wout_ref[D_MODEL:, :], preferred_element_type=F32)

    h = _rms_normed(x, gain_ref[...]).astype(BF16)
    mlp = None
    for c in range(0, MLP_HIDDEN, MLP_HIDDEN_TILE):
        u = jnp.maximum(jnp.dot(h, wup_ref[:, c:c + MLP_HIDDEN_TILE], preferred_element_type=F32), 0.0)
        d = jnp.dot((u * u).astype(BF16), wdn_ref[c:c + MLP_HIDDEN_TILE, :],
                    preferred_element_type=F32)
        mlp = d if mlp is None else mlp + d
    out = x + mlp
    o_ref[...] = _rms_normed(out, fgain_ref[...]) if final else out


def _attn_mlp(x2d, y_mix, xq, kv, w_out, gain, w_up, w_down, final_gain, layer, *, batch, final):
    n = x2d.shape[0]
    tiles = n // batch // ATTN_TILE
    row = lambda width: pl.BlockSpec((ATTN_TILE, width), lambda i: (i, 0))
    heads, _, dv = y_mix.shape
    return pl.pallas_call(
        functools.partial(_attn_mlp_kernel, final=final),
        grid=(n // ATTN_TILE,),
        in_specs=[row(D_MODEL), pl.BlockSpec((heads, ATTN_TILE, dv), lambda i: (0, i, 0)),
                  row(XA_DIM),
                  pl.BlockSpec((1, MEM_LEN, 2 * XA_DIM), lambda i: (layer * batch + i // tiles, 0, 0)),
                  _resident((D_MODEL + XA_DIM, D_MODEL)), _resident((1, D_MODEL)),
                  _resident((D_MODEL, MLP_HIDDEN)), _resident((MLP_HIDDEN, D_MODEL)),
                  _resident((1, D_MODEL))],
        out_specs=row(D_MODEL),
        out_shape=jax.ShapeDtypeStruct((n, D_MODEL), F32),
        compiler_params=_params(("parallel",)),
        name="attn_mlp_final" if final else "attn_mlp",
    )(x2d, y_mix, xq, kv, w_out.astype(BF16), gain.reshape(1, D_MODEL), w_up.astype(BF16),
      w_down.astype(BF16), final_gain.reshape(1, D_MODEL))


def kernel(x, mem, norm_mix, norm_mem, w_kv, w_out, norm_mlp, w_up, w_down,
           gla_w_in, gla_w_gate2, gla_b_gate, gla_out_gain,
           hgrn_w_in, hgrn_lower_bounds, hgrn_out_gain, final_norm):
    batch, seq, _ = x.shape
    assert seq % MIX_TILE == 0 and seq % ROW_TILE == 0 and seq % ATTN_TILE == 0
    x2d = x.reshape(batch * seq, D_MODEL)
    kv = _memkv(mem.reshape(batch * MEM_LEN, D_MODEL), norm_mem, w_kv.astype(BF16))
    kv = kv.reshape(DEPTH * batch, MEM_LEN, 2 * XA_DIM)

    for i in range(DEPTH):
        j = i // 2
        if i % 2 == 0:
            slab, xq = _proj_gla(x2d, norm_mix[i], gla_w_in[j], gla_w_gate2[j], gla_b_gate[j])
            y_mix = _mixer(slab, gla_out_gain[j], batch=batch)
        else:
            slab, xq = _proj_hgrn(x2d, norm_mix[i], hgrn_w_in[j], hgrn_lower_bounds, i)
            y_mix = _mixer(slab, hgrn_out_gain[j], batch=batch)
        x2d = _attn_mlp(x2d, y_mix, xq, kv, w_out[i], norm_mlp[i], w_up[i], w_down[i], final_norm,
                        i, batch=batch, final=(i == DEPTH - 1))
    return x2d.reshape(batch, seq, D_MODEL)
```

```python
import functools

import jax
import jax.numpy as jnp
from jax import lax
from jax.experimental import pallas as pl
from jax.experimental.pallas import tpu as pltpu

F32 = jnp.float32
BF16 = jnp.bfloat16

D_MODEL = 1024
DEPTH = 4
MEM_LEN = 256
NORM_EPS = 1e-6
GLA_HEADS = 4
GLA_DK = 128
GLA_DV = 256
GLA_GATE_RANK = 16
GLA_TAU = 16.0
HGRN_HEADS = 8
HGRN_DF = 128
HGRN_DI = 128
XA_HEADS = 4
XA_DH = 128
XA_DIM = XA_HEADS * XA_DH
MLP_HIDDEN = 4 * D_MODEL

LANES = 128
SUBLANES = 8
V7X_VMEM_BYTES = 64 * 1024 * 1024
VMEM_RESERVE_BYTES = 8 * 1024 * 1024
VMEM_LIMIT_BYTES = V7X_VMEM_BYTES - VMEM_RESERVE_BYTES

CHUNK = 64
ROW_TILE = 1024
ATTN_TILE = 1024
MIX_TILE = 4096
MIX_GROUP = 512
N_CHUNKS = MIX_TILE // CHUNK
GROUP_CHUNKS = MIX_GROUP // CHUNK
LEVELS = (1, 2, 4, 8, 16, 32)
MLP_HIDDEN_TILE = 1024
GATE_PAD = LANES
LOG2E = 1.4426950408889634

_NT = (((1,), (1,)), ((), ()))
_BNT = (((2,), (2,)), ((0,), (0,)))
_BNN = (((2,), (1,)), ((0,), (0,)))
_BTN = (((1,), (1,)), ((0,), (0,)))


def _params(semantics):
    return pltpu.CompilerParams(dimension_semantics=semantics, vmem_limit_bytes=VMEM_LIMIT_BYTES)


def _resident(shape):
    zeros = (0,) * len(shape)
    return pl.BlockSpec(shape, lambda *_: zeros, pipeline_mode=pl.Buffered(1))


def _rms_normed(x, gain):
    ms = jnp.mean(x * x, axis=-1, keepdims=True)
    return x * lax.rsqrt(ms + NORM_EPS) * gain


def _sigmoid(x):
    return 1.0 / (1.0 + jnp.exp(-x))


def _silu(x):
    half = 0.5 * x
    return half * jnp.tanh(half) + half


def _slab_columns(dv):
    q, k, v = 0, LANES, 2 * LANES
    hi = v + dv
    lo = hi + LANES
    gate = lo + LANES
    return q, k, v, hi, lo, gate, gate + dv


def _store_head_slab(slab_ref, head, q, k, v, log_decay, gate):
    dv = v.shape[1]
    g2 = log_decay * LOG2E
    hi = g2.astype(BF16)
    lo = (g2 - hi.astype(F32)).astype(BF16)
    cq, ck, cv, chi, clo, cgate, _ = _slab_columns(dv)
    slab_ref[head, :, cq:cq + LANES] = q.astype(BF16)
    slab_ref[head, :, ck:ck + LANES] = k.astype(BF16)
    slab_ref[head, :, cv:cv + dv] = v.astype(BF16)
    slab_ref[head, :, chi:chi + LANES] = hi
    slab_ref[head, :, clo:clo + LANES] = lo
    slab_ref[head, :, cgate:cgate + dv] = gate.astype(BF16)


def _memkv_kernel(mem_ref, gain_ref, w_ref, o_ref):
    h = _rms_normed(mem_ref[...], gain_ref[0]).astype(BF16)
    o_ref[0] = jnp.dot(h, w_ref[0], preferred_element_type=F32).astype(BF16)


def _memkv(mem2d, norm_mem, w_kv):
    rows = mem2d.shape[0]
    return pl.pallas_call(
        _memkv_kernel,
        grid=(DEPTH,),
        in_specs=[
            _resident((rows, D_MODEL)),
            pl.BlockSpec((1, 1, D_MODEL), lambda l: (l, 0, 0)),
            pl.BlockSpec((1, D_MODEL, 2 * XA_DIM), lambda l: (l, 0, 0)),
        ],
        out_specs=pl.BlockSpec((1, rows, 2 * XA_DIM), lambda l: (l, 0, 0)),
        out_shape=jax.ShapeDtypeStruct((DEPTH, rows, 2 * XA_DIM), BF16),
        compiler_params=_params(("arbitrary",)),
        name="memkv",
    )(mem2d, norm_mem.reshape(DEPTH, 1, D_MODEL), w_kv)


def _proj_out_specs(heads, dv):
    width = _slab_columns(dv)[-1]
    return [pl.BlockSpec((heads, ROW_TILE, width), lambda i: (0, i, 0)),
            pl.BlockSpec((ROW_TILE, XA_DIM), lambda i: (i, 0))]


def _proj_out_shapes(n, heads, dv):
    width = _slab_columns(dv)[-1]
    return [jax.ShapeDtypeStruct((heads, n, width), BF16), jax.ShapeDtypeStruct((n, XA_DIM), BF16)]


_GLA_QK = GLA_HEADS * GLA_DK
_GLA_QKV = 2 * _GLA_QK + D_MODEL
_HGRN_COLS = 4 * D_MODEL + XA_DIM


def _proj_gla_kernel(x_ref, gain_ref, wqkv_ref, wg_ref, wrx_ref, wg2_ref, bg_ref, slab_ref, xq_ref):
    h = _rms_normed(x_ref[...], gain_ref[...]).astype(BF16)
    g_lr = jnp.dot(h, wg_ref[...], preferred_element_type=F32).astype(BF16)
    for head in range(GLA_HEADS):
        lanes = slice(head * LANES, (head + 1) * LANES)
        v0 = 2 * _GLA_QK + head * GLA_DV
        w_head = jnp.concatenate(
            [wqkv_ref[:, lanes], wqkv_ref[:, _GLA_QK + head * GLA_DK:_GLA_QK + (head + 1) * GLA_DK],
             wqkv_ref[:, v0:v0 + GLA_DV], wrx_ref[:, head * GLA_DV:(head + 1) * GLA_DV]], axis=1)
        cols = jnp.dot(h, w_head, preferred_element_type=F32)
        z = jnp.dot(g_lr, wg2_ref[:, lanes], preferred_element_type=F32) + bg_ref[:, lanes]
        log_alpha = (jnp.minimum(z, 0.0) - jnp.log(1.0 + jnp.exp(-jnp.abs(z)))) * (1.0 / GLA_TAU)
        q = cols[:, 0:GLA_DK] * (GLA_DK ** -0.5)
        k = cols[:, GLA_DK:2 * GLA_DK]
        v = cols[:, 2 * GLA_DK:2 * GLA_DK + GLA_DV]
        gate = _silu(cols[:, 2 * GLA_DK + GLA_DV:])
        _store_head_slab(slab_ref, head, q, k, v, log_alpha, gate)
    xq_ref[...] = jnp.dot(h, wrx_ref[:, D_MODEL:], preferred_element_type=F32).astype(BF16)


def _proj_gla(x2d, gain, w_in, w_gate2, b_gate):
    n = x2d.shape[0]
    g0 = _GLA_QKV
    r0 = g0 + GLA_GATE_RANK
    w_qkv = w_in[:, :g0].astype(BF16)
    w_g = jnp.pad(w_in[:, g0:r0], ((0, 0), (0, GATE_PAD - GLA_GATE_RANK))).astype(BF16)
    w_rx = w_in[:, r0:].astype(BF16)
    wg2 = jnp.pad(w_gate2, ((0, GATE_PAD - GLA_GATE_RANK), (0, 0))).astype(BF16)
    return pl.pallas_call(
        _proj_gla_kernel,
        grid=(n // ROW_TILE,),
        in_specs=[pl.BlockSpec((ROW_TILE, D_MODEL), lambda i: (i, 0)), _resident((1, D_MODEL)),
                  _resident((D_MODEL, _GLA_QKV)), _resident((D_MODEL, GATE_PAD)),
                  _resident((D_MODEL, D_MODEL + XA_DIM)), _resident((GATE_PAD, _GLA_QK)),
                  _resident((1, _GLA_QK))],
        out_specs=_proj_out_specs(GLA_HEADS, GLA_DV),
        out_shape=_proj_out_shapes(n, GLA_HEADS, GLA_DV),
        compiler_params=_params(("parallel",)),
        name="proj_gla",
    )(x2d, gain.reshape(1, D_MODEL), w_qkv, w_g, w_rx, wg2, b_gate.reshape(1, _GLA_QK))


def _proj_hgrn_kernel(x_ref, gain_ref, w_ref, lbraw_ref, slab_ref, xq_ref, *, layer):
    h = _rms_normed(x_ref[...], gain_ref[...]).astype(BF16)

    raw = lbraw_ref[...]
    e = jnp.exp(raw - jnp.max(raw, axis=0, keepdims=True))
    p = e / jnp.sum(e, axis=0, keepdims=True)
    lb = jnp.sum(p[1:layer + 1, :], axis=0, keepdims=True)

    for head in range(HGRN_HEADS):
        lanes = slice(head * LANES, (head + 1) * LANES)
        w_head = jnp.concatenate(
            [w_ref[:, g * D_MODEL + head * LANES:g * D_MODEL + (head + 1) * LANES] for g in range(4)],
            axis=1)
        cols = jnp.dot(h, w_head, preferred_element_type=F32)
        q = _silu(cols[:, 0:LANES]) * (HGRN_DF ** -0.5)
        forget = lb[:, lanes] + (1.0 - lb[:, lanes]) * _sigmoid(cols[:, LANES:2 * LANES])
        v = cols[:, 2 * LANES:3 * LANES]
        gate = _silu(cols[:, 3 * LANES:])
        _store_head_slab(slab_ref, head, q, 1.0 - forget, v, jnp.log(forget), gate)
    xq_ref[...] = jnp.dot(h, w_ref[:, 4 * D_MODEL:], preferred_element_type=F32).astype(BF16)


def _proj_hgrn(x2d, gain, w_in, lower_bounds_raw, layer):
    n = x2d.shape[0]
    return pl.pallas_call(
        functools.partial(_proj_hgrn_kernel, layer=layer),
        grid=(n // ROW_TILE,),
        in_specs=[pl.BlockSpec((ROW_TILE, D_MODEL), lambda i: (i, 0)), _resident((1, D_MODEL)),
                  _resident((D_MODEL, _HGRN_COLS)), _resident((DEPTH, D_MODEL))],
        out_specs=_proj_out_specs(HGRN_HEADS, HGRN_DI),
        out_shape=_proj_out_shapes(n, HGRN_HEADS, HGRN_DI),
        compiler_params=_params(("parallel",)),
        name="proj_hgrn",
    )(x2d, gain.reshape(1, D_MODEL), w_in.astype(BF16), lower_bounds_raw)


def _level_rows(b, g2, q, k, n):
    if n >= SUBLANES:
        split = lambda a: a.reshape(MIX_GROUP // (2 * n), 2, n, LANES)
        b4 = split(b)
        r = b4[:, 0, n - 1:n, :]
        lower = split(k)[:, 0] * jnp.exp2(r - b4[:, 0])
        upper = split(q)[:, 1] * jnp.exp2(b4[:, 1] - r)
        return jnp.stack([lower, upper], axis=1).reshape(MIX_GROUP, LANES)
    tile = lambda a: a.reshape(MIX_GROUP // SUBLANES, SUBLANES, LANES)
    sub = lax.broadcasted_iota(jnp.int32, (1, SUBLANES, LANES), 1)
    is_upper = (sub & n) != 0
    if n == 1:
        e = jnp.where(is_upper, tile(g2), 0.0)
    else:
        b3 = tile(b)
        r = b3[:, n - 1:n, :]
        for start in range(2 * n, SUBLANES, 2 * n):
            r = jnp.where(sub >= start, b3[:, start + n - 1:start + n, :], r)
        d = b3 - r
        e = jnp.where(is_upper, d, -d)
    x = jnp.where(is_upper, tile(q), tile(k)) * jnp.exp2(e)
    return x.reshape(MIX_GROUP, LANES)


def _mixer_kernel(slab_ref, gain_ref, o_ref, s_ref, sall_ref, *, dv):
    @pl.when(pl.program_id(2) == 0)
    def _():
        s_ref[...] = jnp.zeros_like(s_ref)

    def tri_masks(n_chunks):
        shape3 = (n_chunks, CHUNK, CHUNK)
        rows = lax.broadcasted_iota(jnp.int32, shape3, 1)
        cols = lax.broadcasted_iota(jnp.int32, shape3, 2)
        return rows >= cols, rows ^ cols

    cq, ck, cv, chi, clo, cgate, _ = _slab_columns(dv)
    causal_all, _ = tri_masks(N_CHUNKS)
    hi_lo = slab_ref[0, :, chi:chi + 2 * LANES]
    bb = lax.dot_general(causal_all.astype(BF16), hi_lo.reshape(N_CHUNKS, CHUNK, 2 * LANES),
                         _BNN, preferred_element_type=F32)
    b_all = (bb[:, :, :LANES] + bb[:, :, LANES:]).reshape(MIX_TILE, LANES)

    causal, sep = tri_masks(1)

    def chunks(a):
        return a.reshape(GROUP_CHUNKS, CHUNK, a.shape[-1])

    s = s_ref[...]
    for group in range(MIX_TILE // MIX_GROUP):
        lo = group * MIX_GROUP
        rs = pl.ds(lo, MIX_GROUP)
        sall = sall_ref.at[pl.ds(group * GROUP_CHUNKS, GROUP_CHUNKS)]
        b = b_all[lo:lo + MIX_GROUP, :]
        g2 = (slab_ref[0, rs, chi:chi + LANES].astype(F32)
              + slab_ref[0, rs, clo:clo + LANES].astype(F32))
        q = slab_ref[0, rs, cq:cq + LANES].astype(F32)
        k = slab_ref[0, rs, ck:ck + LANES].astype(F32)
        v = chunks(slab_ref[0, rs, cv:cv + dv])

        attn = jnp.broadcast_to(jnp.sum(chunks(q * k), axis=-1, keepdims=True),
                                (GROUP_CHUNKS, CHUNK, CHUNK))
        for n in LEVELS:
            x = chunks(_level_rows(b, g2, q, k, n).astype(BF16))
            p = lax.dot_general(x, x, _BNT, preferred_element_type=F32)
            attn = jnp.where(sep >= n, p, attn)
        attn = jnp.where(causal, attn, 0.0).astype(BF16)

        b3 = chunks(b)
        b_last = b3[:, CHUNK - 1:CHUNK, :]
        qd = (chunks(q) * jnp.exp2(b3)).astype(BF16)
        kd = (chunks(k) * jnp.exp2(b_last - b3)).astype(BF16)
        u = lax.dot_general(kd, v, _BTN, preferred_element_type=F32)
        decay = jnp.swapaxes(jnp.broadcast_to(jnp.exp2(b_last), (GROUP_CHUNKS, LANES, LANES)), 1, 2)
        if dv > LANES:
            decay = jnp.concatenate([decay] * (dv // LANES), axis=2)

        for c in range(GROUP_CHUNKS):
            sall[c] = s.astype(BF16)
            s = decay[c] * s + u[c]

        o = (lax.dot_general(attn, v, _BNN, preferred_element_type=F32)
             + lax.dot_general(qd, sall[...], _BNN, preferred_element_type=F32))
        gate = slab_ref[0, rs, cgate:cgate + dv].astype(F32)
        y = _rms_normed(o.reshape(MIX_GROUP, dv), gain_ref[...]) * gate
        o_ref[0, rs, :] = y.astype(o_ref.dtype)
    s_ref[...] = s


def _mixer(slab, out_gain, *, batch):
    heads, n, width = slab.shape
    dv = out_gain.shape[0]
    tiles = n // batch // MIX_TILE
    block = lambda cols: pl.BlockSpec((1, MIX_TILE, cols), lambda b, h, t: (h, b * tiles + t, 0))
    return pl.pallas_call(
        functools.partial(_mixer_kernel, dv=dv),
        grid=(batch, heads, tiles),
        in_specs=[block(width), _resident((1, dv))],
        out_specs=block(dv),
        out_shape=jax.ShapeDtypeStruct((heads, n, dv), BF16),
        scratch_shapes=[pltpu.VMEM((LANES, dv), F32), pltpu.VMEM((N_CHUNKS, LANES, dv), BF16)],
        compiler_params=_params(("parallel", "parallel", "arbitrary")),
        name=f"mixer_dv{dv}",
    )(slab, out_gain.reshape(1, dv))


def _attn_mlp_kernel(x_ref, ymix_ref, xq_ref, kv_ref, wout_ref, gain_ref, wup_ref, wdn_ref,
                     fgain_ref, o_ref, *, final):
    y_mix = jnp.concatenate([ymix_ref[h] for h in range(ymix_ref.shape[0])], axis=1)
    acc = x_ref[...] + jnp.dot(y_mix, wout_ref[0:D_MODEL, :], preferred_element_type=F32)
    xq = xq_ref[...]
    kv = kv_ref[0]
    heads = []
    for h in range(XA_HEADS):
        lo, hi = h * XA_DH, (h + 1) * XA_DH
        s = lax.dot_general(xq[:, lo:hi], kv[:, lo:hi], _NT, preferred_element_type=F32)
        s = s * (XA_DH ** -0.5)
        e = jnp.exp(s - jnp.max(s, axis=-1, keepdims=True)).astype(BF16)
        vh = kv[:, XA_DIM + lo:XA_DIM + hi]
        num_den = jnp.dot(e, jnp.concatenate([vh, jnp.ones_like(vh)], axis=1),
                          preferred_element_type=F32)
        heads.append((num_den[:, :XA_DH] / num_den[:, XA_DH:]).astype(BF16))
    y_mem = jnp.concatenate(heads, axis=1)
    x = acc + jnp.dot(y_mem, wout_ref[D_MODEL:, :], preferred_element_type=F32)

    h = _rms_normed(x, gain_ref[...]).astype(BF16)
    mlp = None
    for c in range(0, MLP_HIDDEN, MLP_HIDDEN_TILE):
        u = jnp.maximum(jnp.dot(h, wup_ref[:, c:c + MLP_HIDDEN_TILE], preferred_element_type=F32), 0.0)
        d = jnp.dot((u * u).astype(BF16), wdn_ref[c:c + MLP_HIDDEN_TILE, :],
                    preferred_element_type=F32)
        mlp = d if mlp is None else mlp + d
    out = x + mlp
    o_ref[...] = _rms_normed(out, fgain_ref[...]) if final else out


def _attn_mlp(x2d, y_mix, xq, kv, w_out, gain, w_up, w_down, final_gain, layer, *, batch, final):
    n = x2d.shape[0]
    tiles = n // batch // ATTN_TILE
    row = lambda width: pl.BlockSpec((ATTN_TILE, width), lambda i: (i, 0))
    heads, _, dv = y_mix.shape
    return pl.pallas_call(
        functools.partial(_attn_mlp_kernel, final=final),
        grid=(n // ATTN_TILE,),
        in_specs=[row(D_MODEL), pl.BlockSpec((heads, ATTN_TILE, dv), lambda i: (0, i, 0)),
                  row(XA_DIM),
                  pl.BlockSpec((1, MEM_LEN, 2 * XA_DIM), lambda i: (layer * batch + i // tiles, 0, 0)),
                  _resident((D_MODEL + XA_DIM, D_MODEL)), _resident((1, D_MODEL)),
                  _resident((D_MODEL, MLP_HIDDEN)), _resident((MLP_HIDDEN, D_MODEL)),
                  _resident((1, D_MODEL))],
        out_specs=row(D_MODEL),
        out_shape=jax.ShapeDtypeStruct((n, D_MODEL), F32),
        compiler_params=_params(("parallel",)),
        name="attn_mlp_final" if final else "attn_mlp",
    )(x2d, y_mix, xq, kv, w_out.astype(BF16), gain.reshape(1, D_MODEL), w_up.astype(BF16),
      w_down.astype(BF16), final_gain.reshape(1, D_MODEL))


def kernel(x, mem, norm_mix, norm_mem, w_kv, w_out, norm_mlp, w_up, w_down,
           gla_w_in, gla_w_gate2, gla_b_gate, gla_out_gain,
           hgrn_w_in, hgrn_lower_bounds, hgrn_out_gain, final_norm):
    batch, seq, _ = x.shape
    assert seq % MIX_TILE == 0 and seq % ROW_TILE == 0 and seq % ATTN_TILE == 0
    x2d = x.reshape(batch * seq, D_MODEL)
    kv = _memkv(mem.reshape(batch * MEM_LEN, D_MODEL), norm_mem, w_kv.astype(BF16))
    kv = kv.reshape(DEPTH * batch, MEM_LEN, 2 * XA_DIM)

    for i in range(DEPTH):
        j = i // 2
        if i % 2 == 0:
            slab, xq = _proj_gla(x2d, norm_mix[i], gla_w_in[j], gla_w_gate2[j], gla_b_gate[j])
            y_mix = _mixer(slab, gla_out_gain[j], batch=batch)
        else:
            slab, xq = _proj_hgrn(x2d, norm_mix[i], hgrn_w_in[j], hgrn_lower_bounds, i)
            y_mix = _mixer(slab, hgrn_out_gain[j], batch=batch)
        x2d = _attn_mlp(x2d, y_mix, xq, kv, w_out[i], norm_mlp[i], w_up[i], w_down[i], final_norm,
                        i, batch=batch, final=(i == DEPTH - 1))
    return x2d.reshape(batch, seq, D_MODEL)
```

```python
import functools

import jax
import jax.numpy as jnp
from jax import lax
from jax.experimental import pallas as pl
from jax.experimental.pallas import tpu as pltpu

F32 = jnp.float32
BF16 = jnp.bfloat16

D_MODEL = 1024
DEPTH = 4
MEM_LEN = 256
NORM_EPS = 1e-6
GLA_HEADS = 4
GLA_DK = 128
GLA_DV = 256
GLA_GATE_RANK = 16
GLA_TAU = 16.0
HGRN_HEADS = 8
HGRN_DF = 128
HGRN_DI = 128
XA_HEADS = 4
XA_DH = 128
XA_DIM = XA_HEADS * XA_DH
MLP_HIDDEN = 4 * D_MODEL

LANES = 128
SUBLANES = 8
V7X_VMEM_BYTES = 64 * 1024 * 1024
VMEM_RESERVE_BYTES = 8 * 1024 * 1024
VMEM_LIMIT_BYTES = V7X_VMEM_BYTES - VMEM_RESERVE_BYTES

CHUNK = 64
ROW_TILE = 1024
ATTN_TILE = 1024
MIX_TILE = 4096
MIX_GROUP = 512
N_CHUNKS = MIX_TILE // CHUNK
GROUP_CHUNKS = MIX_GROUP // CHUNK
LEVELS = (1, 2, 4, 8, 16, 32)
MLP_HIDDEN_TILE = 1024
GATE_PAD = LANES
LOG2E = 1.4426950408889634

_NT = (((1,), (1,)), ((), ()))
_BNT = (((2,), (2,)), ((0,), (0,)))
_BNN = (((2,), (1,)), ((0,), (0,)))
_BTN = (((1,), (1,)), ((0,), (0,)))


def _params(semantics):
    return pltpu.CompilerParams(dimension_semantics=semantics, vmem_limit_bytes=VMEM_LIMIT_BYTES)


def _resident(shape):
    zeros = (0,) * len(shape)
    return pl.BlockSpec(shape, lambda *_: zeros, pipeline_mode=pl.Buffered(1))


def _rms_normed(x, gain):
    ms = jnp.mean(x * x, axis=-1, keepdims=True)
    return x * lax.rsqrt(ms + NORM_EPS) * gain


def _sigmoid(x):
    return 1.0 / (1.0 + jnp.exp(-x))


def _silu(x):
    half = 0.5 * x
    return half * jnp.tanh(half) + half


def _slab_columns(dv):
    q, k, v = 0, LANES, 2 * LANES
    hi = v + dv
    lo = hi + LANES
    gate = lo + LANES
    return q, k, v, hi, lo, gate, gate + dv


def _store_head_slab(slab_ref, head, q, k, v, log_decay, gate):
    dv = v.shape[1]
    g2 = log_decay * LOG2E
    hi = g2.astype(BF16)
    lo = (g2 - hi.astype(F32)).astype(BF16)
    cq, ck, cv, chi, clo, cgate, _ = _slab_columns(dv)
    slab_ref[head, :, cq:cq + LANES] = q.astype(BF16)
    slab_ref[head, :, ck:ck + LANES] = k.astype(BF16)
    slab_ref[head, :, cv:cv + dv] = v.astype(BF16)
    slab_ref[head, :, chi:chi + LANES] = hi
    slab_ref[head, :, clo:clo + LANES] = lo
    slab_ref[head, :, cgate:cgate + dv] = gate.astype(BF16)


def _memkv_kernel(mem_ref, gain_ref, w_ref, o_ref):
    h = _rms_normed(mem_ref[...], gain_ref[0]).astype(BF16)
    o_ref[0] = jnp.dot(h, w_ref[0], preferred_element_type=F32).astype(BF16)


def _memkv(mem2d, norm_mem, w_kv):
    rows = mem2d.shape[0]
    return pl.pallas_call(
        _memkv_kernel,
        grid=(DEPTH,),
        in_specs=[
            _resident((rows, D_MODEL)),
            pl.BlockSpec((1, 1, D_MODEL), lambda l: (l, 0, 0)),
            pl.BlockSpec((1, D_MODEL, 2 * XA_DIM), lambda l: (l, 0, 0)),
        ],
        out_specs=pl.BlockSpec((1, rows, 2 * XA_DIM), lambda l: (l, 0, 0)),
        out_shape=jax.ShapeDtypeStruct((DEPTH, rows, 2 * XA_DIM), BF16),
        compiler_params=_params(("arbitrary",)),
        name="memkv",
    )(mem2d, norm_mem.reshape(DEPTH, 1, D_MODEL), w_kv)


def _proj_out_specs(heads, dv):
    width = _slab_columns(dv)[-1]
    return [pl.BlockSpec((heads, ROW_TILE, width), lambda i: (0, i, 0)),
            pl.BlockSpec((ROW_TILE, XA_DIM), lambda i: (i, 0))]


def _proj_out_shapes(n, heads, dv):
    width = _slab_columns(dv)[-1]
    return [jax.ShapeDtypeStruct((heads, n, width), BF16), jax.ShapeDtypeStruct((n, XA_DIM), BF16)]


_GLA_QK = GLA_HEADS * GLA_DK
_GLA_QKV = 2 * _GLA_QK + D_MODEL
_HGRN_COLS = 4 * D_MODEL + XA_DIM


def _proj_gla_kernel(x_ref, gain_ref, wqkv_ref, wg_ref, wrx_ref, wg2_ref, bg_ref, slab_ref, xq_ref):
    h = _rms_normed(x_ref[...], gain_ref[...]).astype(BF16)
    g_lr = jnp.dot(h, wg_ref[...], preferred_element_type=F32).astype(BF16)
    for head in range(GLA_HEADS):
        lanes = slice(head * LANES, (head + 1) * LANES)
        v0 = 2 * _GLA_QK + head * GLA_DV
        w_head = jnp.concatenate(
            [wqkv_ref[:, lanes], wqkv_ref[:, _GLA_QK + head * GLA_DK:_GLA_QK + (head + 1) * GLA_DK],
             wqkv_ref[:, v0:v0 + GLA_DV], wrx_ref[:, head * GLA_DV:(head + 1) * GLA_DV]], axis=1)
        cols = jnp.dot(h, w_head, preferred_element_type=F32)
        z = jnp.dot(g_lr, wg2_ref[:, lanes], preferred_element_type=F32) + bg_ref[:, lanes]
        log_alpha = (jnp.minimum(z, 0.0) - jnp.log(1.0 + jnp.exp(-jnp.abs(z)))) * (1.0 / GLA_TAU)
        q = cols[:, 0:GLA_DK] * (GLA_DK ** -0.5)
        k = cols[:, GLA_DK:2 * GLA_DK]
        v = cols[:, 2 * GLA_DK:2 * GLA_DK + GLA_DV]
        gate = _silu(cols[:, 2 * GLA_DK + GLA_DV:])
        _store_head_slab(slab_ref, head, q, k, v, log_alpha, gate)
    xq_ref[...] = jnp.dot(h, wrx_ref[:, D_MODEL:], preferred_element_type=F32).astype(BF16)


def _proj_gla(x2d, gain, w_in, w_gate2, b_gate):
    n = x2d.shape[0]
    g0 = _GLA_QKV
    r0 = g0 + GLA_GATE_RANK
    w_qkv = w_in[:, :g0]
    w_g = jnp.pad(w_in[:, g0:r0], ((0, 0), (0, GATE_PAD - GLA_GATE_RANK)))
    w_rx = w_in[:, r0:]
    wg2 = jnp.pad(w_gate2, ((0, GATE_PAD - GLA_GATE_RANK), (0, 0)))
    return pl.pallas_call(
        _proj_gla_kernel,
        grid=(n // ROW_TILE,),
        in_specs=[pl.BlockSpec((ROW_TILE, D_MODEL), lambda i: (i, 0)), _resident((1, D_MODEL)),
                  _resident((D_MODEL, _GLA_QKV)), _resident((D_MODEL, GATE_PAD)),
                  _resident((D_MODEL, D_MODEL + XA_DIM)), _resident((GATE_PAD, _GLA_QK)),
                  _resident((1, _GLA_QK))],
        out_specs=_proj_out_specs(GLA_HEADS, GLA_DV),
        out_shape=_proj_out_shapes(n, GLA_HEADS, GLA_DV),
        compiler_params=_params(("parallel",)),
        name="proj_gla",
    )(x2d, gain.reshape(1, D_MODEL), w_qkv, w_g, w_rx, wg2, b_gate.reshape(1, _GLA_QK))


def _proj_hgrn_kernel(x_ref, gain_ref, w_ref, lbraw_ref, slab_ref, xq_ref, *, layer):
    h = _rms_normed(x_ref[...], gain_ref[...]).astype(BF16)

    raw = lbraw_ref[...]
    e = jnp.exp(raw - jnp.max(raw, axis=0, keepdims=True))
    p = e / jnp.sum(e, axis=0, keepdims=True)
    lb = jnp.sum(p[1:layer + 1, :], axis=0, keepdims=True)

    for head in range(HGRN_HEADS):
        lanes = slice(head * LANES, (head + 1) * LANES)
        w_head = jnp.concatenate(
            [w_ref[:, g * D_MODEL + head * LANES:g * D_MODEL + (head + 1) * LANES] for g in range(4)],
            axis=1)
        cols = jnp.dot(h, w_head, preferred_element_type=F32)
        q = _silu(cols[:, 0:LANES]) * (HGRN_DF ** -0.5)
        forget = lb[:, lanes] + (1.0 - lb[:, lanes]) * _sigmoid(cols[:, LANES:2 * LANES])
        v = cols[:, 2 * LANES:3 * LANES]
        gate = _silu(cols[:, 3 * LANES:])
        _store_head_slab(slab_ref, head, q, 1.0 - forget, v, jnp.log(forget), gate)
    xq_ref[...] = jnp.dot(h, w_ref[:, 4 * D_MODEL:], preferred_element_type=F32).astype(BF16)


def _proj_hgrn(x2d, gain, w_in, lower_bounds_raw, layer):
    n = x2d.shape[0]
    return pl.pallas_call(
        functools.partial(_proj_hgrn_kernel, layer=layer),
        grid=(n // ROW_TILE,),
        in_specs=[pl.BlockSpec((ROW_TILE, D_MODEL), lambda i: (i, 0)), _resident((1, D_MODEL)),
                  _resident((D_MODEL, _HGRN_COLS)), _resident((DEPTH, D_MODEL))],
        out_specs=_proj_out_specs(HGRN_HEADS, HGRN_DI),
        out_shape=_proj_out_shapes(n, HGRN_HEADS, HGRN_DI),
        compiler_params=_params(("parallel",)),
        name="proj_hgrn",
    )(x2d, gain.reshape(1, D_MODEL), w_in, lower_bounds_raw)


def _level_rows(b, g2, q, k, n):
    if n >= SUBLANES:
        split = lambda a: a.reshape(MIX_GROUP // (2 * n), 2, n, LANES)
        b4 = split(b)
        r = b4[:, 0, n - 1:n, :]
        lower = split(k)[:, 0] * jnp.exp2(r - b4[:, 0])
        upper = split(q)[:, 1] * jnp.exp2(b4[:, 1] - r)
        return jnp.stack([lower, upper], axis=1).reshape(MIX_GROUP, LANES)
    tile = lambda a: a.reshape(MIX_GROUP // SUBLANES, SUBLANES, LANES)
    sub = lax.broadcasted_iota(jnp.int32, (1, SUBLANES, LANES), 1)
    is_upper = (sub & n) != 0
    if n == 1:
        e = jnp.where(is_upper, tile(g2), 0.0)
    else:
        b3 = tile(b)
        r = b3[:, n - 1:n, :]
        for start in range(2 * n, SUBLANES, 2 * n):
            r = jnp.where(sub >= start, b3[:, start + n - 1:start + n, :], r)
        d = b3 - r
        e = jnp.where(is_upper, d, -d)
    x = jnp.where(is_upper, tile(q), tile(k)) * jnp.exp2(e)
    return x.reshape(MIX_GROUP, LANES)


def _mixer_kernel(slab_ref, gain_ref, o_ref, s_ref, sall_ref, *, dv):
    @pl.when(pl.program_id(2) == 0)
    def _():
        s_ref[...] = jnp.zeros_like(s_ref)

    def tri_masks(n_chunks):
        shape3 = (n_chunks, CHUNK, CHUNK)
        rows = lax.broadcasted_iota(jnp.int32, shape3, 1)
        cols = lax.broadcasted_iota(jnp.int32, shape3, 2)
        return rows >= cols, rows ^ cols

    cq, ck, cv, chi, clo, cgate, _ = _slab_columns(dv)
    causal_all, _ = tri_masks(N_CHUNKS)
    hi_lo = slab_ref[0, :, chi:chi + 2 * LANES]
    bb = lax.dot_general(causal_all.astype(BF16), hi_lo.reshape(N_CHUNKS, CHUNK, 2 * LANES),
                         _BNN, preferred_element_type=F32)
    b_all = (bb[:, :, :LANES] + bb[:, :, LANES:]).reshape(MIX_TILE, LANES)

    causal, sep = tri_masks(1)

    def chunks(a):
        return a.reshape(GROUP_CHUNKS, CHUNK, a.shape[-1])

    s = s_ref[...]
    for group in range(MIX_TILE // MIX_GROUP):
        lo = group * MIX_GROUP
        rs = pl.ds(lo, MIX_GROUP)
        sall = sall_ref.at[pl.ds(group * GROUP_CHUNKS, GROUP_CHUNKS)]
        b = b_all[lo:lo + MIX_GROUP, :]
        g2 = (slab_ref[0, rs, chi:chi + LANES].astype(F32)
              + slab_ref[0, rs, clo:clo + LANES].astype(F32))
        q = slab_ref[0, rs, cq:cq + LANES].astype(F32)
        k = slab_ref[0, rs, ck:ck + LANES].astype(F32)
        v = chunks(slab_ref[0, rs, cv:cv + dv])

        attn = jnp.broadcast_to(jnp.sum(chunks(q * k), axis=-1, keepdims=True),
                                (GROUP_CHUNKS, CHUNK, CHUNK))
        for n in LEVELS:
            x = chunks(_level_rows(b, g2, q, k, n).astype(BF16))
            p = lax.dot_general(x, x, _BNT, preferred_element_type=F32)
            attn = jnp.where(sep >= n, p, attn)
        attn = jnp.where(causal, attn, 0.0).astype(BF16)

        b3 = chunks(b)
        b_last = b3[:, CHUNK - 1:CHUNK, :]
        qd = (chunks(q) * jnp.exp2(b3)).astype(BF16)
        kd = (chunks(k) * jnp.exp2(b_last - b3)).astype(BF16)
        u = lax.dot_general(kd, v, _BTN, preferred_element_type=F32)
        decay = jnp.swapaxes(jnp.broadcast_to(jnp.exp2(b_last), (GROUP_CHUNKS, LANES, LANES)), 1, 2)
        if dv > LANES:
            decay = jnp.concatenate([decay] * (dv // LANES), axis=2)

        for c in range(GROUP_CHUNKS):
            sall[c] = s.astype(BF16)
            s = decay[c] * s + u[c]

        o = (lax.dot_general(attn, v, _BNN, preferred_element_type=F32)
             + lax.dot_general(qd, sall[...], _BNN, preferred_element_type=F32))
        gate = slab_ref[0, rs, cgate:cgate + dv].astype(F32)
        y = _rms_normed(o.reshape(MIX_GROUP, dv), gain_ref[...]) * gate
        o_ref[0, rs, :] = y.astype(o_ref.dtype)
    s_ref[...] = s


def _mixer(slab, out_gain, *, batch):
    heads, n, width = slab.shape
    dv = out_gain.shape[0]
    tiles = n // batch // MIX_TILE
    block = lambda cols: pl.BlockSpec((1, MIX_TILE, cols), lambda b, h, t: (h, b * tiles + t, 0))
    return pl.pallas_call(
        functools.partial(_mixer_kernel, dv=dv),
        grid=(batch, heads, tiles),
        in_specs=[block(width), _resident((1, dv))],
        out_specs=block(dv),
        out_shape=jax.ShapeDtypeStruct((heads, n, dv), BF16),
        scratch_shapes=[pltpu.VMEM((LANES, dv), F32), pltpu.VMEM((N_CHUNKS, LANES, dv), BF16)],
        compiler_params=_params(("parallel", "parallel", "arbitrary")),
        name=f"mixer_dv{dv}",
    )(slab, out_gain.reshape(1, dv))


def _attn_mlp_kernel(x_ref, ymix_ref, xq_ref, kv_ref, wout_ref, gain_ref, wup_ref, wdn_ref,
                     fgain_ref, o_ref, *, final):
    y_mix = jnp.concatenate([ymix_ref[h] for h in range(ymix_ref.shape[0])], axis=1)
    acc = x_ref[...] + jnp.dot(y_mix, wout_ref[0:D_MODEL, :], preferred_element_type=F32)
    xq = xq_ref[...]
    kv = kv_ref[0]
    heads = []
    for h in range(XA_HEADS):
        lo, hi = h * XA_DH, (h + 1) * XA_DH
        s = lax.dot_general(xq[:, lo:hi], kv[:, lo:hi], _NT, preferred_element_type=F32)
        s = s * (XA_DH ** -0.5)
        e = jnp.exp(s - jnp.max(s, axis=-1, keepdims=True)).astype(BF16)
        vh = kv[:, XA_DIM + lo:XA_DIM + hi]
        num_den = jnp.dot(e, jnp.concatenate([vh, jnp.ones_like(vh)], axis=1),
                          preferred_element_type=F32)
        heads.append((num_den[:, :XA_DH] / num_den[:, XA_DH:]).astype(BF16))
    y_mem = jnp.concatenate(heads, axis=1)
    x = acc + jnp.dot(y_mem, wout_ref[D_MODEL:, :], preferred_element_type=F32)

    h = _rms_normed(x, gain_ref[...]).astype(BF16)
    mlp = None
    for c in range(0, MLP_HIDDEN, MLP_HIDDEN_TILE):
        u = jnp.maximum(jnp.dot(h, wup_ref[:, c:c + MLP_HIDDEN_TILE], preferred_element_type=F32), 0.0)
        d = jnp.dot((u * u).astype(BF16), wdn_ref[c:c + MLP_HIDDEN_TILE, :],
                    preferred_element_type=F32)
        mlp = d if mlp is None else mlp + d
    out = x + mlp
    o_ref[...] = _rms_normed(out, fgain_ref[...]) if final else out


def _attn_mlp(x2d, y_mix, xq, kv, w_out, gain, w_up, w_down, final_gain, layer, *, batch, final):
    n = x2d.shape[0]
    tiles = n // batch // ATTN_TILE
    row = lambda width: pl.BlockSpec((ATTN_TILE, width), lambda i: (i, 0))
    heads, _, dv = y_mix.shape
    return pl.pallas_call(
        functools.partial(_attn_mlp_kernel, final=final),
        grid=(n // ATTN_TILE,),
        in_specs=[row(D_MODEL), pl.BlockSpec((heads, ATTN_TILE, dv), lambda i: (0, i, 0)),
                  row(XA_DIM),
                  pl.BlockSpec((1, MEM_LEN, 2 * XA_DIM), lambda i: (layer * batch + i // tiles, 0, 0)),
                  _resident((D_MODEL + XA_DIM, D_MODEL)), _resident((1, D_MODEL)),
                  _resident((D_MODEL, MLP_HIDDEN)), _resident((MLP_HIDDEN, D_MODEL)),
                  _resident((1, D_MODEL))],
        out_specs=row(D_MODEL),
        out_shape=jax.ShapeDtypeStruct((n, D_MODEL), F32),
        compiler_params=_params(("parallel",)),
        name="attn_mlp_final" if final else "attn_mlp",
    )(x2d, y_mix, xq, kv, w_out, gain.reshape(1, D_MODEL), w_up, w_down,
      final_gain.reshape(1, D_MODEL))


def kernel(x, mem, norm_mix, norm_mem, w_kv, w_out, norm_mlp, w_up, w_down,
           gla_w_in, gla_w_gate2, gla_b_gate, gla_out_gain,
           hgrn_w_in, hgrn_lower_bounds, hgrn_out_gain, final_norm):
    batch, seq, _ = x.shape
    assert seq % MIX_TILE == 0 and seq % ROW_TILE == 0 and seq % ATTN_TILE == 0
    x2d = x.reshape(batch * seq, D_MODEL)
    w_kv, w_out, w_up, w_down, gla_w_in, gla_w_gate2, hgrn_w_in = (
        a.astype(BF16) for a in (w_kv, w_out, w_up, w_down, gla_w_in, gla_w_gate2, hgrn_w_in))
    kv = _memkv(mem.reshape(batch * MEM_LEN, D_MODEL), norm_mem, w_kv)
    kv = kv.reshape(DEPTH * batch, MEM_LEN, 2 * XA_DIM)

    for i in range(DEPTH):
        j = i // 2
        if i % 2 == 0:
            slab, xq = _proj_gla(x2d, norm_mix[i], gla_w_in[j], gla_w_gate2[j], gla_b_gate[j])
            y_mix = _mixer(slab, gla_out_gain[j], batch=batch)
        else:
            slab, xq = _proj_hgrn(x2d, norm_mix[i], hgrn_w_in[j], hgrn_lower_bounds, i)
            y_mix = _mixer(slab, hgrn_out_gain[j], batch=batch)
        x2d = _attn_mlp(x2d, y_mix, xq, kv, w_out[i], norm_mlp[i], w_up[i], w_down[i], final_norm,
                        i, batch=batch, final=(i == DEPTH - 1))
    return x2d.reshape(batch, seq, D_MODEL)
```

```python
import functools

import jax
import jax.numpy as jnp
from jax import lax
from jax.experimental import pallas as pl
from jax.experimental.pallas import tpu as pltpu

F32 = jnp.float32
BF16 = jnp.bfloat16

D_MODEL = 1024
DEPTH = 4
MEM_LEN = 256
NORM_EPS = 1e-6
GLA_HEADS = 4
GLA_DK = 128
GLA_DV = 256
GLA_GATE_RANK = 16
GLA_TAU = 16.0
HGRN_HEADS = 8
HGRN_DF = 128
HGRN_DI = 128
XA_HEADS = 4
XA_DH = 128
XA_DIM = XA_HEADS * XA_DH
MLP_HIDDEN = 4 * D_MODEL

LANES = 128
SUBLANES = 8
V7X_VMEM_BYTES = 64 * 1024 * 1024
VMEM_RESERVE_BYTES = 8 * 1024 * 1024
VMEM_LIMIT_BYTES = V7X_VMEM_BYTES - VMEM_RESERVE_BYTES

CHUNK = 64
ROW_TILE = 1024
ATTN_TILE = 1024
MIX_TILE = {HGRN_DI: 8192, GLA_DV: 4096}
MIX_GROUP = 512
GROUP_CHUNKS = MIX_GROUP // CHUNK
LEVELS = (1, 2, 4, 8, 16, 32)
MLP_HIDDEN_TILE = 1024
GATE_PAD = LANES
LOG2E = 1.4426950408889634

_NT = (((1,), (1,)), ((), ()))
_BNT = (((2,), (2,)), ((0,), (0,)))
_BNN = (((2,), (1,)), ((0,), (0,)))
_BTN = (((1,), (1,)), ((0,), (0,)))


def _params(semantics):
    return pltpu.CompilerParams(dimension_semantics=semantics, vmem_limit_bytes=VMEM_LIMIT_BYTES)


def _resident(shape):
    zeros = (0,) * len(shape)
    return pl.BlockSpec(shape, lambda *_: zeros, pipeline_mode=pl.Buffered(1))


def _rms_normed(x, gain):
    ms = jnp.mean(x * x, axis=-1, keepdims=True)
    return x * lax.rsqrt(ms + NORM_EPS) * gain


def _sigmoid(x):
    return 1.0 / (1.0 + jnp.exp(-x))


def _silu(x):
    half = 0.5 * x
    return half * jnp.tanh(half) + half


def _slab_columns(dv):
    q, k, v = 0, LANES, 2 * LANES
    hi = v + dv
    lo = hi + LANES
    gate = lo + LANES
    return q, k, v, hi, lo, gate, gate + dv


def _store_head_slab(slab_ref, head, q, k, v, log_decay, gate):
    dv = v.shape[1]
    g2 = log_decay * LOG2E
    hi = g2.astype(BF16)
    lo = (g2 - hi.astype(F32)).astype(BF16)
    cq, ck, cv, chi, clo, cgate, _ = _slab_columns(dv)
    slab_ref[head, :, cq:cq + LANES] = q.astype(BF16)
    slab_ref[head, :, ck:ck + LANES] = k.astype(BF16)
    slab_ref[head, :, cv:cv + dv] = v.astype(BF16)
    slab_ref[head, :, chi:chi + LANES] = hi
    slab_ref[head, :, clo:clo + LANES] = lo
    slab_ref[head, :, cgate:cgate + dv] = gate.astype(BF16)


def _memkv_kernel(mem_ref, gain_ref, w_ref, o_ref):
    h = _rms_normed(mem_ref[...], gain_ref[0]).astype(BF16)
    o_ref[0] = jnp.dot(h, w_ref[0], preferred_element_type=F32).astype(BF16)


def _memkv(mem2d, norm_mem, w_kv):
    rows = mem2d.shape[0]
    return pl.pallas_call(
        _memkv_kernel,
        grid=(DEPTH,),
        in_specs=[
            _resident((rows, D_MODEL)),
            pl.BlockSpec((1, 1, D_MODEL), lambda l: (l, 0, 0)),
            pl.BlockSpec((1, D_MODEL, 2 * XA_DIM), lambda l: (l, 0, 0)),
        ],
        out_specs=pl.BlockSpec((1, rows, 2 * XA_DIM), lambda l: (l, 0, 0)),
        out_shape=jax.ShapeDtypeStruct((DEPTH, rows, 2 * XA_DIM), BF16),
        compiler_params=_params(("arbitrary",)),
        name="memkv",
    )(mem2d, norm_mem.reshape(DEPTH, 1, D_MODEL), w_kv)


def _proj_out_specs(heads, dv):
    width = _slab_columns(dv)[-1]
    return [pl.BlockSpec((heads, ROW_TILE, width), lambda i: (0, i, 0)),
            pl.BlockSpec((ROW_TILE, XA_DIM), lambda i: (i, 0))]


def _proj_out_shapes(n, heads, dv):
    width = _slab_columns(dv)[-1]
    return [jax.ShapeDtypeStruct((heads, n, width), BF16), jax.ShapeDtypeStruct((n, XA_DIM), BF16)]


_GLA_QK = GLA_HEADS * GLA_DK
_GLA_QKV = 2 * _GLA_QK + D_MODEL
_HGRN_COLS = 4 * D_MODEL + XA_DIM


def _proj_gla_kernel(x_ref, gain_ref, wqkv_ref, wg_ref, wrx_ref, wg2_ref, bg_ref, slab_ref, xq_ref):
    h = _rms_normed(x_ref[...], gain_ref[...]).astype(BF16)
    g_lr = jnp.dot(h, wg_ref[...], preferred_element_type=F32).astype(BF16)
    for head in range(GLA_HEADS):
        lanes = slice(head * LANES, (head + 1) * LANES)
        v0 = 2 * _GLA_QK + head * GLA_DV
        w_head = jnp.concatenate(
            [wqkv_ref[:, lanes], wqkv_ref[:, _GLA_QK + head * GLA_DK:_GLA_QK + (head + 1) * GLA_DK],
             wqkv_ref[:, v0:v0 + GLA_DV], wrx_ref[:, head * GLA_DV:(head + 1) * GLA_DV]], axis=1)
        cols = jnp.dot(h, w_head, preferred_element_type=F32)
        z = jnp.dot(g_lr, wg2_ref[:, lanes], preferred_element_type=F32) + bg_ref[:, lanes]
        log_alpha = (jnp.minimum(z, 0.0) - jnp.log(1.0 + jnp.exp(-jnp.abs(z)))) * (1.0 / GLA_TAU)
        q = cols[:, 0:GLA_DK] * (GLA_DK ** -0.5)
        k = cols[:, GLA_DK:2 * GLA_DK]
        v = cols[:, 2 * GLA_DK:2 * GLA_DK + GLA_DV]
        gate = _silu(cols[:, 2 * GLA_DK + GLA_DV:])
        _store_head_slab(slab_ref, head, q, k, v, log_alpha, gate)
    xq_ref[...] = jnp.dot(h, wrx_ref[:, D_MODEL:], preferred_element_type=F32).astype(BF16)


def _proj_gla(x2d, gain, w_in, w_gate2, b_gate):
    n = x2d.shape[0]
    g0 = _GLA_QKV
    r0 = g0 + GLA_GATE_RANK
    w_qkv = w_in[:, :g0]
    w_g = jnp.pad(w_in[:, g0:r0], ((0, 0), (0, GATE_PAD - GLA_GATE_RANK)))
    w_rx = w_in[:, r0:]
    wg2 = jnp.pad(w_gate2, ((0, GATE_PAD - GLA_GATE_RANK), (0, 0)))
    return pl.pallas_call(
        _proj_gla_kernel,
        grid=(n // ROW_TILE,),
        in_specs=[pl.BlockSpec((ROW_TILE, D_MODEL), lambda i: (i, 0)), _resident((1, D_MODEL)),
                  _resident((D_MODEL, _GLA_QKV)), _resident((D_MODEL, GATE_PAD)),
                  _resident((D_MODEL, D_MODEL + XA_DIM)), _resident((GATE_PAD, _GLA_QK)),
                  _resident((1, _GLA_QK))],
        out_specs=_proj_out_specs(GLA_HEADS, GLA_DV),
        out_shape=_proj_out_shapes(n, GLA_HEADS, GLA_DV),
        compiler_params=_params(("parallel",)),
        name="proj_gla",
    )(x2d, gain.reshape(1, D_MODEL), w_qkv, w_g, w_rx, wg2, b_gate.reshape(1, _GLA_QK))


def _proj_hgrn_kernel(x_ref, gain_ref, w_ref, lbraw_ref, slab_ref, xq_ref, *, layer):
    h = _rms_normed(x_ref[...], gain_ref[...]).astype(BF16)

    raw = lbraw_ref[...]
    e = jnp.exp(raw - jnp.max(raw, axis=0, keepdims=True))
    p = e / jnp.sum(e, axis=0, keepdims=True)
    lb = jnp.sum(p[1:layer + 1, :], axis=0, keepdims=True)

    for head in range(HGRN_HEADS):
        lanes = slice(head * LANES, (head + 1) * LANES)
        w_head = jnp.concatenate(
            [w_ref[:, g * D_MODEL + head * LANES:g * D_MODEL + (head + 1) * LANES] for g in range(4)],
            axis=1)
        cols = jnp.dot(h, w_head, preferred_element_type=F32)
        q = _silu(cols[:, 0:LANES]) * (HGRN_DF ** -0.5)
        forget = lb[:, lanes] + (1.0 - lb[:, lanes]) * _sigmoid(cols[:, LANES:2 * LANES])
        v = cols[:, 2 * LANES:3 * LANES]
        gate = _silu(cols[:, 3 * LANES:])
        _store_head_slab(slab_ref, head, q, 1.0 - forget, v, jnp.log(forget), gate)
    xq_ref[...] = jnp.dot(h, w_ref[:, 4 * D_MODEL:], preferred_element_type=F32).astype(BF16)


def _proj_hgrn(x2d, gain, w_in, lower_bounds_raw, layer):
    n = x2d.shape[0]
    return pl.pallas_call(
        functools.partial(_proj_hgrn_kernel, layer=layer),
        grid=(n // ROW_TILE,),
        in_specs=[pl.BlockSpec((ROW_TILE, D_MODEL), lambda i: (i, 0)), _resident((1, D_MODEL)),
                  _resident((D_MODEL, _HGRN_COLS)), _resident((DEPTH, D_MODEL))],
        out_specs=_proj_out_specs(HGRN_HEADS, HGRN_DI),
        out_shape=_proj_out_shapes(n, HGRN_HEADS, HGRN_DI),
        compiler_params=_params(("parallel",)),
        name="proj_hgrn",
    )(x2d, gain.reshape(1, D_MODEL), w_in, lower_bounds_raw)


def _level_rows(b, g2, q, k, n):
    if n >= SUBLANES:
        split = lambda a: a.reshape(MIX_GROUP // (2 * n), 2, n, LANES)
        b4 = split(b)
        r = b4[:, 0, n - 1:n, :]
        lower = split(k)[:, 0] * jnp.exp2(r - b4[:, 0])
        upper = split(q)[:, 1] * jnp.exp2(b4[:, 1] - r)
        return jnp.stack([lower, upper], axis=1).reshape(MIX_GROUP, LANES)
    tile = lambda a: a.reshape(MIX_GROUP // SUBLANES, SUBLANES, LANES)
    sub = lax.broadcasted_iota(jnp.int32, (1, SUBLANES, LANES), 1)
    is_upper = (sub & n) != 0
    if n == 1:
        e = jnp.where(is_upper, tile(g2), 0.0)
    else:
        b3 = tile(b)
        r = b3[:, n - 1:n, :]
        for start in range(2 * n, SUBLANES, 2 * n):
            r = jnp.where(sub >= start, b3[:, start + n - 1:start + n, :], r)
        d = b3 - r
        e = jnp.where(is_upper, d, -d)
    x = jnp.where(is_upper, tile(q), tile(k)) * jnp.exp2(e)
    return x.reshape(MIX_GROUP, LANES)


def _mixer_kernel(slab_ref, gain_ref, o_ref, s_ref, sall_ref, *, dv):
    @pl.when(pl.program_id(2) == 0)
    def _():
        s_ref[...] = jnp.zeros_like(s_ref)

    def tri_masks(n_chunks):
        shape3 = (n_chunks, CHUNK, CHUNK)
        rows = lax.broadcasted_iota(jnp.int32, shape3, 1)
        cols = lax.broadcasted_iota(jnp.int32, shape3, 2)
        return rows >= cols, rows ^ cols

    cq, ck, cv, chi, clo, cgate, _ = _slab_columns(dv)
    tile = slab_ref.shape[1]
    causal_all, _ = tri_masks(tile // CHUNK)
    hi_lo = slab_ref[0, :, chi:chi + 2 * LANES]
    bb = lax.dot_general(causal_all.astype(BF16), hi_lo.reshape(tile // CHUNK, CHUNK, 2 * LANES),
                         _BNN, preferred_element_type=F32)
    b_all = (bb[:, :, :LANES] + bb[:, :, LANES:]).reshape(tile, LANES)

    causal, sep = tri_masks(1)

    def chunks(a):
        return a.reshape(GROUP_CHUNKS, CHUNK, a.shape[-1])

    s = s_ref[...]
    for group in range(tile // MIX_GROUP):
        lo = group * MIX_GROUP
        rs = pl.ds(lo, MIX_GROUP)
        sall = sall_ref.at[pl.ds(group * GROUP_CHUNKS, GROUP_CHUNKS)]
        b = b_all[lo:lo + MIX_GROUP, :]
        g2 = (slab_ref[0, rs, chi:chi + LANES].astype(F32)
              + slab_ref[0, rs, clo:clo + LANES].astype(F32))
        q = slab_ref[0, rs, cq:cq + LANES].astype(F32)
        k = slab_ref[0, rs, ck:ck + LANES].astype(F32)
        v = chunks(slab_ref[0, rs, cv:cv + dv])

        attn = jnp.broadcast_to(jnp.sum(chunks(q * k), axis=-1, keepdims=True),
                                (GROUP_CHUNKS, CHUNK, CHUNK))
        for n in LEVELS:
            x = chunks(_level_rows(b, g2, q, k, n).astype(BF16))
            p = lax.dot_general(x, x, _BNT, preferred_element_type=F32)
            attn = jnp.where(sep >= n, p, attn)
        attn = jnp.where(causal, attn, 0.0).astype(BF16)

        b3 = chunks(b)
        b_last = b3[:, CHUNK - 1:CHUNK, :]
        qd = (chunks(q) * jnp.exp2(b3)).astype(BF16)
        kd = (chunks(k) * jnp.exp2(b_last - b3)).astype(BF16)
        u = lax.dot_general(kd, v, _BTN, preferred_element_type=F32)
        decay = jnp.swapaxes(jnp.broadcast_to(jnp.exp2(b_last), (GROUP_CHUNKS, LANES, LANES)), 1, 2)
        if dv > LANES:
            decay = jnp.concatenate([decay] * (dv // LANES), axis=2)

        for c in range(GROUP_CHUNKS):
            sall[c] = s.astype(BF16)
            s = decay[c] * s + u[c]

        o = (lax.dot_general(attn, v, _BNN, preferred_element_type=F32)
             + lax.dot_general(qd, sall[...], _BNN, preferred_element_type=F32))
        gate = slab_ref[0, rs, cgate:cgate + dv].astype(F32)
        y = _rms_normed(o.reshape(MIX_GROUP, dv), gain_ref[...]) * gate
        o_ref[0, rs, :] = y.astype(o_ref.dtype)
    s_ref[...] = s


def _mixer(slab, out_gain, *, batch):
    heads, n, width = slab.shape
    dv = out_gain.shape[0]
    tile = MIX_TILE[dv]
    assert (n // batch) % tile == 0
    tiles = n // batch // tile
    block = lambda cols: pl.BlockSpec((1, tile, cols), lambda b, h, t: (h, b * tiles + t, 0))
    return pl.pallas_call(
        functools.partial(_mixer_kernel, dv=dv),
        grid=(batch, heads, tiles),
        in_specs=[block(width), _resident((1, dv))],
        out_specs=block(dv),
        out_shape=jax.ShapeDtypeStruct((heads, n, dv), BF16),
        scratch_shapes=[pltpu.VMEM((LANES, dv), F32), pltpu.VMEM((tile // CHUNK, LANES, dv), BF16)],
        compiler_params=_params(("parallel", "parallel", "arbitrary")),
        name=f"mixer_dv{dv}",
    )(slab, out_gain.reshape(1, dv))


def _attn_mlp_kernel(x_ref, ymix_ref, xq_ref, kv_ref, wout_ref, gain_ref, wup_ref, wdn_ref,
                     fgain_ref, o_ref, *, final):
    y_mix = jnp.concatenate([ymix_ref[h] for h in range(ymix_ref.shape[0])], axis=1)
    acc = x_ref[...] + jnp.dot(y_mix, wout_ref[0:D_MODEL, :], preferred_element_type=F32)
    xq = xq_ref[...]
    kv = kv_ref[0]
    heads = []
    for h in range(XA_HEADS):
        lo, hi = h * XA_DH, (h + 1) * XA_DH
        s = lax.dot_general(xq[:, lo:hi], kv[:, lo:hi], _NT, preferred_element_type=F32)
        s = s * (XA_DH ** -0.5)
        e = jnp.exp(s - jnp.max(s, axis=-1, keepdims=True)).astype(BF16)
        vh = kv[:, XA_DIM + lo:XA_DIM + hi]
        num_den = jnp.dot(e, jnp.concatenate([vh, jnp.ones_like(vh)], axis=1),
                          preferred_element_type=F32)
        heads.append((num_den[:, :XA_DH] / num_den[:, XA_DH:]).astype(BF16))
    y_mem = jnp.concatenate(heads, axis=1)
    x = acc + jnp.dot(y_mem, wout_ref[D_MODEL:, :], preferred_element_type=F32)

    h = _rms_normed(x, gain_ref[...]).astype(BF16)
    mlp = None
    for c in range(0, MLP_HIDDEN, MLP_HIDDEN_TILE):
        u = jnp.maximum(jnp.dot(h, wup_ref[:, c:c + MLP_HIDDEN_TILE], preferred_element_type=F32), 0.0)
        d = jnp.dot((u * u).astype(BF16), wdn_ref[c:c + MLP_HIDDEN_TILE, :],
                    preferred_element_type=F32)
        mlp = d if mlp is None else mlp + d
    out = x + mlp
    o_ref[...] = _rms_normed(out, fgain_ref[...]) if final else out


def _attn_mlp(x2d, y_mix, xq, kv, w_out, gain, w_up, w_down, final_gain, layer, *, batch, final):
    n = x2d.shape[0]
    tiles = n // batch // ATTN_TILE
    row = lambda width: pl.BlockSpec((ATTN_TILE, width), lambda i: (i, 0))
    heads, _, dv = y_mix.shape
    return pl.pallas_call(
        functools.partial(_attn_mlp_kernel, final=final),
        grid=(n // ATTN_TILE,),
        in_specs=[row(D_MODEL), pl.BlockSpec((heads, ATTN_TILE, dv), lambda i: (0, i, 0)),
                  row(XA_DIM),
                  pl.BlockSpec((1, MEM_LEN, 2 * XA_DIM), lambda i: (layer * batch + i // tiles, 0, 0)),
                  _resident((D_MODEL + XA_DIM, D_MODEL)), _resident((1, D_MODEL)),
                  _resident((D_MODEL, MLP_HIDDEN)), _resident((MLP_HIDDEN, D_MODEL)),
                  _resident((1, D_MODEL))],
        out_specs=row(D_MODEL),
        out_shape=jax.ShapeDtypeStruct((n, D_MODEL), F32),
        compiler_params=_params(("parallel",)),
        name="attn_mlp_final" if final else "attn_mlp",
    )(x2d, y_mix, xq, kv, w_out, gain.reshape(1, D_MODEL), w_up, w_down,
      final_gain.reshape(1, D_MODEL))


def kernel(x, mem, norm_mix, norm_mem, w_kv, w_out, norm_mlp, w_up, w_down,
           gla_w_in, gla_w_gate2, gla_b_gate, gla_out_gain,
           hgrn_w_in, hgrn_lower_bounds, hgrn_out_gain, final_norm):
    batch, seq, _ = x.shape
    assert seq % ROW_TILE == 0 and seq % ATTN_TILE == 0
    x2d = x.reshape(batch * seq, D_MODEL)
    w_kv, w_out, w_up, w_down, gla_w_in, gla_w_gate2, hgrn_w_in = (
        a.astype(BF16) for a in (w_kv, w_out, w_up, w_down, gla_w_in, gla_w_gate2, hgrn_w_in))
    kv = _memkv(mem.reshape(batch * MEM_LEN, D_MODEL), norm_mem, w_kv)
    kv = kv.reshape(DEPTH * batch, MEM_LEN, 2 * XA_DIM)

    for i in range(DEPTH):
        j = i // 2
        if i % 2 == 0:
            slab, xq = _proj_gla(x2d, norm_mix[i], gla_w_in[j], gla_w_gate2[j], gla_b_gate[j])
            y_mix = _mixer(slab, gla_out_gain[j], batch=batch)
        else:
            slab, xq = _proj_hgrn(x2d, norm_mix[i], hgrn_w_in[j], hgrn_lower_bounds, i)
            y_mix = _mixer(slab, hgrn_out_gain[j], batch=batch)
        x2d = _attn_mlp(x2d, y_mix, xq, kv, w_out[i], norm_mlp[i], w_up[i], w_down[i], final_norm,
                        i, batch=batch, final=(i == DEPTH - 1))
    return x2d.reshape(batch, seq, D_MODEL)
```

```python
import functools

import jax
import jax.numpy as jnp
from jax import lax
from jax.experimental import pallas as pl
from jax.experimental.pallas import tpu as pltpu

F32 = jnp.float32
BF16 = jnp.bfloat16

D_MODEL = 1024
DEPTH = 4
MEM_LEN = 256
NORM_EPS = 1e-6
GLA_HEADS = 4
GLA_DK = 128
GLA_DV = 256
GLA_GATE_RANK = 16
GLA_TAU = 16.0
HGRN_HEADS = 8
HGRN_DF = 128
HGRN_DI = 128
XA_HEADS = 4
XA_DH = 128
XA_DIM = XA_HEADS * XA_DH
MLP_HIDDEN = 4 * D_MODEL

LANES = 128
SUBLANES = 8
V7X_VMEM_BYTES = 64 * 1024 * 1024
VMEM_RESERVE_BYTES = 8 * 1024 * 1024
VMEM_LIMIT_BYTES = V7X_VMEM_BYTES - VMEM_RESERVE_BYTES

CHUNK = 64
ROW_TILE = 1024
ATTN_TILE = 1024
MIX_TILE = {HGRN_DI: 8192, GLA_DV: 4096}
MIX_GROUP = 512
GROUP_CHUNKS = MIX_GROUP // CHUNK
LEVELS = (1, 2, 4, 8, 16, 32)
MLP_HIDDEN_TILE = 1024
GATE_PAD = LANES
LOG2E = 1.4426950408889634

_NT = (((1,), (1,)), ((), ()))
_BNT = (((2,), (2,)), ((0,), (0,)))
_BNN = (((2,), (1,)), ((0,), (0,)))
_BTN = (((1,), (1,)), ((0,), (0,)))


def _params(semantics):
    return pltpu.CompilerParams(dimension_semantics=semantics, vmem_limit_bytes=VMEM_LIMIT_BYTES)


def _resident(shape):
    zeros = (0,) * len(shape)
    return pl.BlockSpec(shape, lambda *_: zeros, pipeline_mode=pl.Buffered(1))


def _rms_normed(x, gain):
    ms = jnp.mean(x * x, axis=-1, keepdims=True)
    return x * lax.rsqrt(ms + NORM_EPS) * gain


def _sigmoid(x):
    return 1.0 / (1.0 + jnp.exp(-x))


def _silu(x):
    half = 0.5 * x
    return half * jnp.tanh(half) + half


def _slab_columns(dv):
    q, k, v = 0, LANES, 2 * LANES
    hi = v + dv
    lo = hi + LANES
    gate = lo + LANES
    return q, k, v, hi, lo, gate, gate + dv


def _store_head_slab(slab_ref, head, q, k, v, log_decay, gate):
    dv = v.shape[1]
    g2 = log_decay * LOG2E
    hi = g2.astype(BF16)
    lo = (g2 - hi.astype(F32)).astype(BF16)
    cq, ck, cv, chi, clo, cgate, _ = _slab_columns(dv)
    slab_ref[head, :, cq:cq + LANES] = q.astype(BF16)
    slab_ref[head, :, ck:ck + LANES] = k.astype(BF16)
    slab_ref[head, :, cv:cv + dv] = v.astype(BF16)
    slab_ref[head, :, chi:chi + LANES] = hi
    slab_ref[head, :, clo:clo + LANES] = lo
    slab_ref[head, :, cgate:cgate + dv] = gate.astype(BF16)


def _memkv_kernel(mem_ref, gain_ref, w_ref, o_ref):
    h = _rms_normed(mem_ref[...], gain_ref[0]).astype(BF16)
    o_ref[0] = jnp.dot(h, w_ref[0], preferred_element_type=F32).astype(BF16)


def _memkv(mem2d, norm_mem, w_kv):
    rows = mem2d.shape[0]
    return pl.pallas_call(
        _memkv_kernel,
        grid=(DEPTH,),
        in_specs=[
            _resident((rows, D_MODEL)),
            pl.BlockSpec((1, 1, D_MODEL), lambda l: (l, 0, 0)),
            pl.BlockSpec((1, D_MODEL, 2 * XA_DIM), lambda l: (l, 0, 0)),
        ],
        out_specs=pl.BlockSpec((1, rows, 2 * XA_DIM), lambda l: (l, 0, 0)),
        out_shape=jax.ShapeDtypeStruct((DEPTH, rows, 2 * XA_DIM), BF16),
        compiler_params=_params(("arbitrary",)),
        name="memkv",
    )(mem2d, norm_mem.reshape(DEPTH, 1, D_MODEL), w_kv)


def _proj_out_specs(heads, dv):
    width = _slab_columns(dv)[-1]
    return [pl.BlockSpec((heads, ROW_TILE, width), lambda i: (0, i, 0)),
            pl.BlockSpec((ROW_TILE, XA_DIM), lambda i: (i, 0))]


def _proj_out_shapes(n, heads, dv):
    width = _slab_columns(dv)[-1]
    return [jax.ShapeDtypeStruct((heads, n, width), BF16), jax.ShapeDtypeStruct((n, XA_DIM), BF16)]


_GLA_QK = GLA_HEADS * GLA_DK
_GLA_QKV = 2 * _GLA_QK + D_MODEL
_HGRN_COLS = 4 * D_MODEL + XA_DIM


def _proj_gla_kernel(x_ref, gain_ref, wqkv_ref, wg_ref, wrx_ref, wg2_ref, bg_ref, slab_ref, xq_ref):
    h = _rms_normed(x_ref[...], gain_ref[...]).astype(BF16)
    g_lr = jnp.dot(h, wg_ref[...], preferred_element_type=F32).astype(BF16)
    for head in range(GLA_HEADS):
        lanes = slice(head * LANES, (head + 1) * LANES)
        v0 = 2 * _GLA_QK + head * GLA_DV
        w_head = jnp.concatenate(
            [wqkv_ref[:, lanes], wqkv_ref[:, _GLA_QK + head * GLA_DK:_GLA_QK + (head + 1) * GLA_DK],
             wqkv_ref[:, v0:v0 + GLA_DV], wrx_ref[:, head * GLA_DV:(head + 1) * GLA_DV]], axis=1)
        cols = jnp.dot(h, w_head, preferred_element_type=F32)
        z = jnp.dot(g_lr, wg2_ref[:, lanes], preferred_element_type=F32) + bg_ref[:, lanes]
        log_alpha = (jnp.minimum(z, 0.0) - jnp.log(1.0 + jnp.exp(-jnp.abs(z)))) * (1.0 / GLA_TAU)
        q = cols[:, 0:GLA_DK] * (GLA_DK ** -0.5)
        k = cols[:, GLA_DK:2 * GLA_DK]
        v = cols[:, 2 * GLA_DK:2 * GLA_DK + GLA_DV]
        gate = _silu(cols[:, 2 * GLA_DK + GLA_DV:])
        _store_head_slab(slab_ref, head, q, k, v, log_alpha, gate)
    xq_ref[...] = jnp.dot(h, wrx_ref[:, D_MODEL:], preferred_element_type=F32).astype(BF16)


def _proj_gla(x2d, gain, w_in, w_gate2, b_gate):
    n = x2d.shape[0]
    g0 = _GLA_QKV
    r0 = g0 + GLA_GATE_RANK
    w_qkv = w_in[:, :g0]
    w_g = jnp.pad(w_in[:, g0:r0], ((0, 0), (0, GATE_PAD - GLA_GATE_RANK)))
    w_rx = w_in[:, r0:]
    wg2 = jnp.pad(w_gate2, ((0, GATE_PAD - GLA_GATE_RANK), (0, 0)))
    return pl.pallas_call(
        _proj_gla_kernel,
        grid=(n // ROW_TILE,),
        in_specs=[pl.BlockSpec((ROW_TILE, D_MODEL), lambda i: (i, 0)), _resident((1, D_MODEL)),
                  _resident((D_MODEL, _GLA_QKV)), _resident((D_MODEL, GATE_PAD)),
                  _resident((D_MODEL, D_MODEL + XA_DIM)), _resident((GATE_PAD, _GLA_QK)),
                  _resident((1, _GLA_QK))],
        out_specs=_proj_out_specs(GLA_HEADS, GLA_DV),
        out_shape=_proj_out_shapes(n, GLA_HEADS, GLA_DV),
        compiler_params=_params(("parallel",)),
        name="proj_gla",
    )(x2d, gain.reshape(1, D_MODEL), w_qkv, w_g, w_rx, wg2, b_gate.reshape(1, _GLA_QK))


def _proj_hgrn_kernel(x_ref, gain_ref, w_ref, lbraw_ref, slab_ref, xq_ref, *, layer):
    h = _rms_normed(x_ref[...], gain_ref[...]).astype(BF16)

    raw = lbraw_ref[...]
    e = jnp.exp(raw - jnp.max(raw, axis=0, keepdims=True))
    p = e / jnp.sum(e, axis=0, keepdims=True)
    lb = jnp.sum(p[1:layer + 1, :], axis=0, keepdims=True)

    for head in range(HGRN_HEADS):
        lanes = slice(head * LANES, (head + 1) * LANES)
        w_head = jnp.concatenate(
            [w_ref[:, g * D_MODEL + head * LANES:g * D_MODEL + (head + 1) * LANES] for g in range(4)],
            axis=1)
        cols = jnp.dot(h, w_head, preferred_element_type=F32)
        q = _silu(cols[:, 0:LANES]) * (HGRN_DF ** -0.5)
        forget = lb[:, lanes] + (1.0 - lb[:, lanes]) * _sigmoid(cols[:, LANES:2 * LANES])
        v = cols[:, 2 * LANES:3 * LANES]
        gate = _silu(cols[:, 3 * LANES:])
        _store_head_slab(slab_ref, head, q, 1.0 - forget, v, jnp.log(forget), gate)
    xq_ref[...] = jnp.dot(h, w_ref[:, 4 * D_MODEL:], preferred_element_type=F32).astype(BF16)


def _proj_hgrn(x2d, gain, w_in, lower_bounds_raw, layer):
    n = x2d.shape[0]
    return pl.pallas_call(
        functools.partial(_proj_hgrn_kernel, layer=layer),
        grid=(n // ROW_TILE,),
        in_specs=[pl.BlockSpec((ROW_TILE, D_MODEL), lambda i: (i, 0)), _resident((1, D_MODEL)),
                  _resident((D_MODEL, _HGRN_COLS)), _resident((DEPTH, D_MODEL))],
        out_specs=_proj_out_specs(HGRN_HEADS, HGRN_DI),
        out_shape=_proj_out_shapes(n, HGRN_HEADS, HGRN_DI),
        compiler_params=_params(("parallel",)),
        name="proj_hgrn",
    )(x2d, gain.reshape(1, D_MODEL), w_in, lower_bounds_raw)


def _level_rows(b, q, k, n):
    if n >= SUBLANES:
        split = lambda a: a.reshape(MIX_GROUP // (2 * n), 2, n, LANES)
        b4 = split(b)
        r = b4[:, 0, n - 1:n, :]
        lower = split(k)[:, 0] * jnp.exp2(r - b4[:, 0])
        upper = split(q)[:, 1] * jnp.exp2(b4[:, 1] - r)
        return jnp.stack([lower, upper], axis=1).reshape(MIX_GROUP, LANES)
    tile = lambda a: a.reshape(MIX_GROUP // SUBLANES, SUBLANES, LANES)
    sub = lax.broadcasted_iota(jnp.int32, (1, SUBLANES, LANES), 1)
    is_upper = (sub & n) != 0
    b3 = tile(b)
    r = b3[:, n - 1:n, :]
    for start in range(2 * n, SUBLANES, 2 * n):
        r = jnp.where(sub >= start, b3[:, start + n - 1:start + n, :], r)
    d = b3 - r
    e = jnp.where(is_upper, d, -d)
    x = jnp.where(is_upper, tile(q), tile(k)) * jnp.exp2(e)
    return x.reshape(MIX_GROUP, LANES)


def _mixer_kernel(slab_ref, gain_ref, o_ref, s_ref, sall_ref, *, dv):
    @pl.when(pl.program_id(2) == 0)
    def _():
        s_ref[...] = jnp.zeros_like(s_ref)

    def tri_masks(n_chunks):
        shape3 = (n_chunks, CHUNK, CHUNK)
        rows = lax.broadcasted_iota(jnp.int32, shape3, 1)
        cols = lax.broadcasted_iota(jnp.int32, shape3, 2)
        return rows >= cols, rows ^ cols

    cq, ck, cv, chi, clo, cgate, _ = _slab_columns(dv)
    tile = slab_ref.shape[1]
    causal_all, _ = tri_masks(tile // CHUNK)
    hi_lo = slab_ref[0, :, chi:chi + 2 * LANES]
    bb = lax.dot_general(causal_all.astype(BF16), hi_lo.reshape(tile // CHUNK, CHUNK, 2 * LANES),
                         _BNN, preferred_element_type=F32)
    b_all = (bb[:, :, :LANES] + bb[:, :, LANES:]).reshape(tile, LANES)

    causal, sep = tri_masks(1)

    def chunks(a):
        return a.reshape(GROUP_CHUNKS, CHUNK, a.shape[-1])

    s = s_ref[...]
    for group in range(tile // MIX_GROUP):
        lo = group * MIX_GROUP
        rs = pl.ds(lo, MIX_GROUP)
        sall = sall_ref.at[pl.ds(group * GROUP_CHUNKS, GROUP_CHUNKS)]
        b = b_all[lo:lo + MIX_GROUP, :]
        g2 = (slab_ref[0, rs, chi:chi + LANES].astype(F32)
              + slab_ref[0, rs, clo:clo + LANES].astype(F32))
        q = slab_ref[0, rs, cq:cq + LANES].astype(F32)
        k = slab_ref[0, rs, ck:ck + LANES].astype(F32)
        v = chunks(slab_ref[0, rs, cv:cv + dv])

        attn = jnp.broadcast_to(jnp.sum(chunks(q * k), axis=-1, keepdims=True),
                                (GROUP_CHUNKS, CHUNK, CHUNK))
        rows8 = lambda a: a.reshape(MIX_GROUP // SUBLANES, SUBLANES, LANES)
        k_prev = pltpu.roll(rows8(k), 1, 1).reshape(MIX_GROUP, LANES)
        d1 = jnp.sum(chunks(q * jnp.exp2(g2) * k_prev), axis=-1, keepdims=True)
        attn = jnp.where(sep >= 1, d1, attn)
        for n in LEVELS[1:]:
            x = chunks(_level_rows(b, q, k, n).astype(BF16))
            p = lax.dot_general(x, x, _BNT, preferred_element_type=F32)
            attn = jnp.where(sep >= n, p, attn)
        attn = jnp.where(causal, attn, 0.0).astype(BF16)

        b3 = chunks(b)
        b_last = b3[:, CHUNK - 1:CHUNK, :]
        qd = (chunks(q) * jnp.exp2(b3)).astype(BF16)
        kd = (chunks(k) * jnp.exp2(b_last - b3)).astype(BF16)
        u = lax.dot_general(kd, v, _BTN, preferred_element_type=F32)
        decay = jnp.swapaxes(jnp.broadcast_to(jnp.exp2(b_last), (GROUP_CHUNKS, LANES, LANES)), 1, 2)
        if dv > LANES:
            decay = jnp.concatenate([decay] * (dv // LANES), axis=2)

        for c in range(GROUP_CHUNKS):
            sall[c] = s.astype(BF16)
            s = decay[c] * s + u[c]

        o = (lax.dot_general(attn, v, _BNN, preferred_element_type=F32)
             + lax.dot_general(qd, sall[...], _BNN, preferred_element_type=F32))
        gate = slab_ref[0, rs, cgate:cgate + dv].astype(F32)
        y = _rms_normed(o.reshape(MIX_GROUP, dv), gain_ref[...]) * gate
        o_ref[0, rs, :] = y.astype(o_ref.dtype)
    s_ref[...] = s


def _mixer(slab, out_gain, *, batch):
    heads, n, width = slab.shape
    dv = out_gain.shape[0]
    tile = MIX_TILE[dv]
    assert (n // batch) % tile == 0
    tiles = n // batch // tile
    block = lambda cols: pl.BlockSpec((1, tile, cols), lambda b, h, t: (h, b * tiles + t, 0))
    return pl.pallas_call(
        functools.partial(_mixer_kernel, dv=dv),
        grid=(batch, heads, tiles),
        in_specs=[block(width), _resident((1, dv))],
        out_specs=block(dv),
        out_shape=jax.ShapeDtypeStruct((heads, n, dv), BF16),
        scratch_shapes=[pltpu.VMEM((LANES, dv), F32), pltpu.VMEM((tile // CHUNK, LANES, dv), BF16)],
        compiler_params=_params(("parallel", "parallel", "arbitrary")),
        name=f"mixer_dv{dv}",
    )(slab, out_gain.reshape(1, dv))


def _attn_mlp_kernel(x_ref, ymix_ref, xq_ref, kv_ref, wout_ref, gain_ref, wup_ref, wdn_ref,
                     fgain_ref, o_ref, *, final):
    y_mix = jnp.concatenate([ymix_ref[h] for h in range(ymix_ref.shape[0])], axis=1)
    acc = x_ref[...] + jnp.dot(y_mix, wout_ref[0:D_MODEL, :], preferred_element_type=F32)
    xq = xq_ref[...]
    kv = kv_ref[0]
    heads = []
    for h in range(XA_HEADS):
        lo, hi = h * XA_DH, (h + 1) * XA_DH
        s = lax.dot_general(xq[:, lo:hi], kv[:, lo:hi], _NT, preferred_element_type=F32)
        s = s * (XA_DH ** -0.5)
        e = jnp.exp(s - jnp.max(s, axis=-1, keepdims=True)).astype(BF16)
        vh = kv[:, XA_DIM + lo:XA_DIM + hi]
        num_den = jnp.dot(e, jnp.concatenate([vh, jnp.ones_like(vh)], axis=1),
                          preferred_element_type=F32)
        heads.append((num_den[:, :XA_DH] / num_den[:, XA_DH:]).astype(BF16))
    y_mem = jnp.concatenate(heads, axis=1)
    x = acc + jnp.dot(y_mem, wout_ref[D_MODEL:, :], preferred_element_type=F32)

    h = _rms_normed(x, gain_ref[...]).astype(BF16)
    mlp = None
    for c in range(0, MLP_HIDDEN, MLP_HIDDEN_TILE):
        u = jnp.maximum(jnp.dot(h, wup_ref[:, c:c + MLP_HIDDEN_TILE], preferred_element_type=F32), 0.0)
        d = jnp.dot((u * u).astype(BF16), wdn_ref[c:c + MLP_HIDDEN_TILE, :],
                    preferred_element_type=F32)
        mlp = d if mlp is None else mlp + d
    out = x + mlp
    o_ref[...] = _rms_normed(out, fgain_ref[...]) if final else out


def _attn_mlp(x2d, y_mix, xq, kv, w_out, gain, w_up, w_down, final_gain, layer, *, batch, final):
    n = x2d.shape[0]
    tiles = n // batch // ATTN_TILE
    row = lambda width: pl.BlockSpec((ATTN_TILE, width), lambda i: (i, 0))
    heads, _, dv = y_mix.shape
    return pl.pallas_call(
        functools.partial(_attn_mlp_kernel, final=final),
        grid=(n // ATTN_TILE,),
        in_specs=[row(D_MODEL), pl.BlockSpec((heads, ATTN_TILE, dv), lambda i: (0, i, 0)),
                  row(XA_DIM),
                  pl.BlockSpec((1, MEM_LEN, 2 * XA_DIM), lambda i: (layer * batch + i // tiles, 0, 0)),
                  _resident((D_MODEL + XA_DIM, D_MODEL)), _resident((1, D_MODEL)),
                  _resident((D_MODEL, MLP_HIDDEN)), _resident((MLP_HIDDEN, D_MODEL)),
                  _resident((1, D_MODEL))],
        out_specs=row(D_MODEL),
        out_shape=jax.ShapeDtypeStruct((n, D_MODEL), F32),
        compiler_params=_params(("parallel",)),
        name="attn_mlp_final" if final else "attn_mlp",
    )(x2d, y_mix, xq, kv, w_out, gain.reshape(1, D_MODEL), w_up, w_down,
      final_gain.reshape(1, D_MODEL))


def kernel(x, mem, norm_mix, norm_mem, w_kv, w_out, norm_mlp, w_up, w_down,
           gla_w_in, gla_w_gate2, gla_b_gate, gla_out_gain,
           hgrn_w_in, hgrn_lower_bounds, hgrn_out_gain, final_norm):
    batch, seq, _ = x.shape
    assert seq % ROW_TILE == 0 and seq % ATTN_TILE == 0
    x2d = x.reshape(batch * seq, D_MODEL)
    w_kv, w_out, w_up, w_down, gla_w_in, gla_w_gate2, hgrn_w_in = (
        a.astype(BF16) for a in (w_kv, w_out, w_up, w_down, gla_w_in, gla_w_gate2, hgrn_w_in))
    kv = _memkv(mem.reshape(batch * MEM_LEN, D_MODEL), norm_mem, w_kv)
    kv = kv.reshape(DEPTH * batch, MEM_LEN, 2 * XA_DIM)

    for i in range(DEPTH):
        j = i // 2
        if i % 2 == 0:
            slab, xq = _proj_gla(x2d, norm_mix[i], gla_w_in[j], gla_w_gate2[j], gla_b_gate[j])
            y_mix = _mixer(slab, gla_out_gain[j], batch=batch)
        else:
            slab, xq = _proj_hgrn(x2d, norm_mix[i], hgrn_w_in[j], hgrn_lower_bounds, i)
            y_mix = _mixer(slab, hgrn_out_gain[j], batch=batch)
        x2d = _attn_mlp(x2d, y_mix, xq, kv, w_out[i], norm_mlp[i], w_up[i], w_down[i], final_norm,
                        i, batch=batch, final=(i == DEPTH - 1))
    return x2d.reshape(batch, seq, D_MODEL)
```

```python
import functools

import jax
import jax.numpy as jnp
from jax import lax
from jax.experimental import pallas as pl
from jax.experimental.pallas import tpu as pltpu

F32 = jnp.float32
BF16 = jnp.bfloat16

D_MODEL = 1024
DEPTH = 4
MEM_LEN = 256
NORM_EPS = 1e-6
GLA_HEADS = 4
GLA_DK = 128
GLA_DV = 256
GLA_GATE_RANK = 16
GLA_TAU = 16.0
HGRN_HEADS = 8
HGRN_DF = 128
HGRN_DI = 128
XA_HEADS = 4
XA_DH = 128
XA_DIM = XA_HEADS * XA_DH
MLP_HIDDEN = 4 * D_MODEL

LANES = 128
SUBLANES = 8
V7X_VMEM_BYTES = 64 * 1024 * 1024
VMEM_RESERVE_BYTES = 8 * 1024 * 1024
VMEM_LIMIT_BYTES = V7X_VMEM_BYTES - VMEM_RESERVE_BYTES

CHUNK = 64
ROW_TILE = 1024
ATTN_TILE = 1024
MIX_TILE = {HGRN_DI: 8192, GLA_DV: 4096}
MIX_GROUP = 512
GROUP_CHUNKS = MIX_GROUP // CHUNK
LEVELS = (1, 2, 4, 8, 16, 32)
MLP_HIDDEN_TILE = 1024
GATE_PAD = LANES
LOG2E = 1.4426950408889634

_NT = (((1,), (1,)), ((), ()))
_BNT = (((2,), (2,)), ((0,), (0,)))
_BNN = (((2,), (1,)), ((0,), (0,)))
_BTN = (((1,), (1,)), ((0,), (0,)))


def _params(semantics):
    return pltpu.CompilerParams(dimension_semantics=semantics, vmem_limit_bytes=VMEM_LIMIT_BYTES)


def _resident(shape):
    zeros = (0,) * len(shape)
    return pl.BlockSpec(shape, lambda *_: zeros, pipeline_mode=pl.Buffered(1))


def _rms_normed(x, gain):
    ms = jnp.mean(x * x, axis=-1, keepdims=True)
    return x * lax.rsqrt(ms + NORM_EPS) * gain


def _sigmoid(x):
    return 1.0 / (1.0 + jnp.exp(-x))


def _silu(x):
    half = 0.5 * x
    return half * jnp.tanh(half) + half


def _slab_columns(dv):
    q, k, v = 0, LANES, 2 * LANES
    hi = v + dv
    lo = hi + LANES
    gate = lo + LANES
    return q, k, v, hi, lo, gate, gate + dv


def _store_head_slab(slab_ref, head, q, k, v, log_decay, gate):
    dv = v.shape[1]
    g2 = log_decay * LOG2E
    hi = g2.astype(BF16)
    lo = (g2 - hi.astype(F32)).astype(BF16)
    cq, ck, cv, chi, clo, cgate, _ = _slab_columns(dv)
    slab_ref[head, :, cq:cq + LANES] = q.astype(BF16)
    slab_ref[head, :, ck:ck + LANES] = k.astype(BF16)
    slab_ref[head, :, cv:cv + dv] = v.astype(BF16)
    slab_ref[head, :, chi:chi + LANES] = hi
    slab_ref[head, :, clo:clo + LANES] = lo
    slab_ref[head, :, cgate:cgate + dv] = gate.astype(BF16)


def _memkv_kernel(mem_ref, gain_ref, w_ref, o_ref):
    h = _rms_normed(mem_ref[...], gain_ref[0]).astype(BF16)
    o_ref[0] = jnp.dot(h, w_ref[0], preferred_element_type=F32).astype(BF16)


def _memkv(mem2d, norm_mem, w_kv):
    rows = mem2d.shape[0]
    return pl.pallas_call(
        _memkv_kernel,
        grid=(DEPTH,),
        in_specs=[
            _resident((rows, D_MODEL)),
            pl.BlockSpec((1, 1, D_MODEL), lambda l: (l, 0, 0)),
            pl.BlockSpec((1, D_MODEL, 2 * XA_DIM), lambda l: (l, 0, 0)),
        ],
        out_specs=pl.BlockSpec((1, rows, 2 * XA_DIM), lambda l: (l, 0, 0)),
        out_shape=jax.ShapeDtypeStruct((DEPTH, rows, 2 * XA_DIM), BF16),
        compiler_params=_params(("arbitrary",)),
        name="memkv",
    )(mem2d, norm_mem.reshape(DEPTH, 1, D_MODEL), w_kv)


def _proj_out_specs(heads, dv):
    width = _slab_columns(dv)[-1]
    return [pl.BlockSpec((heads, ROW_TILE, width), lambda i: (0, i, 0)),
            pl.BlockSpec((ROW_TILE, XA_DIM), lambda i: (i, 0))]


def _proj_out_shapes(n, heads, dv):
    width = _slab_columns(dv)[-1]
    return [jax.ShapeDtypeStruct((heads, n, width), BF16), jax.ShapeDtypeStruct((n, XA_DIM), BF16)]


_GLA_QK = GLA_HEADS * GLA_DK
_GLA_QKV = 2 * _GLA_QK + D_MODEL
_HGRN_COLS = 4 * D_MODEL + XA_DIM


def _proj_gla_kernel(x_ref, gain_ref, wqkv_ref, wg_ref, wrx_ref, wg2_ref, bg_ref, slab_ref, xq_ref):
    h = _rms_normed(x_ref[...], gain_ref[...]).astype(BF16)
    g_lr = jnp.dot(h, wg_ref[...], preferred_element_type=F32).astype(BF16)
    for head in range(GLA_HEADS):
        lanes = slice(head * LANES, (head + 1) * LANES)
        v0 = 2 * _GLA_QK + head * GLA_DV
        w_head = jnp.concatenate(
            [wqkv_ref[:, lanes], wqkv_ref[:, _GLA_QK + head * GLA_DK:_GLA_QK + (head + 1) * GLA_DK],
             wqkv_ref[:, v0:v0 + GLA_DV], wrx_ref[:, head * GLA_DV:(head + 1) * GLA_DV]], axis=1)
        cols = jnp.dot(h, w_head, preferred_element_type=F32)
        z = jnp.dot(g_lr, wg2_ref[:, lanes], preferred_element_type=F32) + bg_ref[:, lanes]
        log_alpha = (jnp.minimum(z, 0.0) - jnp.log(1.0 + jnp.exp(-jnp.abs(z)))) * (1.0 / GLA_TAU)
        q = cols[:, 0:GLA_DK] * (GLA_DK ** -0.5)
        k = cols[:, GLA_DK:2 * GLA_DK]
        v = cols[:, 2 * GLA_DK:2 * GLA_DK + GLA_DV]
        gate = _silu(cols[:, 2 * GLA_DK + GLA_DV:])
        _store_head_slab(slab_ref, head, q, k, v, log_alpha, gate)
    xq_ref[...] = jnp.dot(h, wrx_ref[:, D_MODEL:], preferred_element_type=F32).astype(BF16)


def _proj_gla(x2d, gain, w_in, w_gate2, b_gate):
    n = x2d.shape[0]
    g0 = _GLA_QKV
    r0 = g0 + GLA_GATE_RANK
    w_qkv = w_in[:, :g0]
    w_g = jnp.pad(w_in[:, g0:r0], ((0, 0), (0, GATE_PAD - GLA_GATE_RANK)))
    w_rx = w_in[:, r0:]
    wg2 = jnp.pad(w_gate2, ((0, GATE_PAD - GLA_GATE_RANK), (0, 0)))
    return pl.pallas_call(
        _proj_gla_kernel,
        grid=(n // ROW_TILE,),
        in_specs=[pl.BlockSpec((ROW_TILE, D_MODEL), lambda i: (i, 0)), _resident((1, D_MODEL)),
                  _resident((D_MODEL, _GLA_QKV)), _resident((D_MODEL, GATE_PAD)),
                  _resident((D_MODEL, D_MODEL + XA_DIM)), _resident((GATE_PAD, _GLA_QK)),
                  _resident((1, _GLA_QK))],
        out_specs=_proj_out_specs(GLA_HEADS, GLA_DV),
        out_shape=_proj_out_shapes(n, GLA_HEADS, GLA_DV),
        compiler_params=_params(("parallel",)),
        name="proj_gla",
    )(x2d, gain.reshape(1, D_MODEL), w_qkv, w_g, w_rx, wg2, b_gate.reshape(1, _GLA_QK))


def _proj_hgrn_kernel(x_ref, gain_ref, w_ref, lbraw_ref, slab_ref, xq_ref, *, layer):
    h = _rms_normed(x_ref[...], gain_ref[...]).astype(BF16)

    raw = lbraw_ref[...]
    e = jnp.exp(raw - jnp.max(raw, axis=0, keepdims=True))
    p = e / jnp.sum(e, axis=0, keepdims=True)
    lb = jnp.sum(p[1:layer + 1, :], axis=0, keepdims=True)

    for head in range(HGRN_HEADS):
        lanes = slice(head * LANES, (head + 1) * LANES)
        w_head = jnp.concatenate(
            [w_ref[:, g * D_MODEL + head * LANES:g * D_MODEL + (head + 1) * LANES] for g in range(4)],
            axis=1)
        cols = jnp.dot(h, w_head, preferred_element_type=F32)
        q = _silu(cols[:, 0:LANES]) * (HGRN_DF ** -0.5)
        forget = lb[:, lanes] + (1.0 - lb[:, lanes]) * _sigmoid(cols[:, LANES:2 * LANES])
        v = cols[:, 2 * LANES:3 * LANES]
        gate = _silu(cols[:, 3 * LANES:])
        _store_head_slab(slab_ref, head, q, 1.0 - forget, v, jnp.log(forget), gate)
    xq_ref[...] = jnp.dot(h, w_ref[:, 4 * D_MODEL:], preferred_element_type=F32).astype(BF16)


def _proj_hgrn(x2d, gain, w_in, lower_bounds_raw, layer):
    n = x2d.shape[0]
    return pl.pallas_call(
        functools.partial(_proj_hgrn_kernel, layer=layer),
        grid=(n // ROW_TILE,),
        in_specs=[pl.BlockSpec((ROW_TILE, D_MODEL), lambda i: (i, 0)), _resident((1, D_MODEL)),
                  _resident((D_MODEL, _HGRN_COLS)), _resident((DEPTH, D_MODEL))],
        out_specs=_proj_out_specs(HGRN_HEADS, HGRN_DI),
        out_shape=_proj_out_shapes(n, HGRN_HEADS, HGRN_DI),
        compiler_params=_params(("parallel",)),
        name="proj_hgrn",
    )(x2d, gain.reshape(1, D_MODEL), w_in, lower_bounds_raw)


def _level_rows(b, q, k, n):
    if n >= SUBLANES:
        split = lambda a: a.reshape(MIX_GROUP // (2 * n), 2, n, LANES)
        b4 = split(b)
        r = b4[:, 0, n - 1:n, :]
        lower = split(k)[:, 0] * jnp.exp2(r - b4[:, 0])
        upper = split(q)[:, 1] * jnp.exp2(b4[:, 1] - r)
        return jnp.stack([lower, upper], axis=1).reshape(MIX_GROUP, LANES)
    tile = lambda a: a.reshape(MIX_GROUP // SUBLANES, SUBLANES, LANES)
    sub = lax.broadcasted_iota(jnp.int32, (1, SUBLANES, LANES), 1)
    is_upper = (sub & n) != 0
    b3 = tile(b)
    r = b3[:, n - 1:n, :]
    for start in range(2 * n, SUBLANES, 2 * n):
        r = jnp.where(sub >= start, b3[:, start + n - 1:start + n, :], r)
    d = b3 - r
    e = jnp.where(is_upper, d, -d)
    x = jnp.where(is_upper, tile(q), tile(k)) * jnp.exp2(e)
    return x.reshape(MIX_GROUP, LANES)


def _mixer_kernel(slab_ref, gain_ref, o_ref, s_ref, sall_ref, *, dv):
    @pl.when(pl.program_id(2) == 0)
    def _():
        s_ref[...] = jnp.zeros_like(s_ref)

    def tri_masks(n_chunks):
        shape3 = (n_chunks, CHUNK, CHUNK)
        rows = lax.broadcasted_iota(jnp.int32, shape3, 1)
        cols = lax.broadcasted_iota(jnp.int32, shape3, 2)
        return rows >= cols, rows ^ cols

    cq, ck, cv, chi, clo, cgate, _ = _slab_columns(dv)
    tile = slab_ref.shape[1]
    causal_all, _ = tri_masks(tile // CHUNK)
    hi_lo = slab_ref[0, :, chi:chi + 2 * LANES]
    bb = lax.dot_general(causal_all.astype(BF16), hi_lo.reshape(tile // CHUNK, CHUNK, 2 * LANES),
                         _BNN, preferred_element_type=F32)
    b_all = (bb[:, :, :LANES] + bb[:, :, LANES:]).reshape(tile, LANES)

    causal, sep = tri_masks(1)

    def chunks(a):
        return a.reshape(GROUP_CHUNKS, CHUNK, a.shape[-1])

    s = s_ref[...]
    for group in range(tile // MIX_GROUP):
        lo = group * MIX_GROUP
        rs = pl.ds(lo, MIX_GROUP)
        sall = sall_ref.at[pl.ds(group * GROUP_CHUNKS, GROUP_CHUNKS)]
        b = b_all[lo:lo + MIX_GROUP, :]
        g2 = (slab_ref[0, rs, chi:chi + LANES].astype(F32)
              + slab_ref[0, rs, clo:clo + LANES].astype(F32))
        q = slab_ref[0, rs, cq:cq + LANES].astype(F32)
        k = slab_ref[0, rs, ck:ck + LANES].astype(F32)
        v = chunks(slab_ref[0, rs, cv:cv + dv])

        attn = jnp.broadcast_to(jnp.sum(chunks(q * k), axis=-1, keepdims=True),
                                (GROUP_CHUNKS, CHUNK, CHUNK))
        rows8 = lambda a: a.reshape(MIX_GROUP // SUBLANES, SUBLANES, LANES)
        k_prev = pltpu.roll(rows8(k), 1, 1).reshape(MIX_GROUP, LANES)
        d1 = jnp.sum(chunks(q * jnp.exp2(g2) * k_prev), axis=-1, keepdims=True)
        attn = jnp.where(sep >= 1, d1, attn)
        for n in LEVELS[1:]:
            x = chunks(_level_rows(b, q, k, n).astype(BF16))
            p = lax.dot_general(x, x, _BNT, preferred_element_type=F32)
            attn = jnp.where(sep >= n, p, attn)
        attn = jnp.where(causal, attn, 0.0).astype(BF16)

        b3 = chunks(b)
        b_last = b3[:, CHUNK - 1:CHUNK, :]
        qd = (chunks(q) * jnp.exp2(b3)).astype(BF16)
        kd = (chunks(k) * jnp.exp2(b_last - b3)).astype(BF16)
        u = lax.dot_general(kd, v, _BTN, preferred_element_type=F32)
        decay = jnp.swapaxes(jnp.broadcast_to(jnp.exp2(b_last), (GROUP_CHUNKS, LANES, LANES)), 1, 2)
        if dv > LANES:
            decay = jnp.concatenate([decay] * (dv // LANES), axis=2)

        for c in range(GROUP_CHUNKS):
            sall[c] = s.astype(BF16)
            s = decay[c] * s + u[c]

        o = (lax.dot_general(attn, v, _BNN, preferred_element_type=F32)
             + lax.dot_general(qd, sall[...], _BNN, preferred_element_type=F32))
        gate = slab_ref[0, rs, cgate:cgate + dv].astype(F32)
        o2 = o.reshape(MIX_GROUP, dv)
        ms = jnp.dot((o2 * o2).astype(BF16), jnp.ones((dv, LANES), BF16),
                     preferred_element_type=F32) * (1.0 / dv)
        inv = lax.rsqrt(ms + NORM_EPS)
        if dv > LANES:
            inv = jnp.concatenate([inv] * (dv // LANES), axis=1)
        y = o2 * inv * gain_ref[...] * gate
        o_ref[0, rs, :] = y.astype(o_ref.dtype)
    s_ref[...] = s


def _mixer(slab, out_gain, *, batch):
    heads, n, width = slab.shape
    dv = out_gain.shape[0]
    tile = MIX_TILE[dv]
    assert (n // batch) % tile == 0
    tiles = n // batch // tile
    block = lambda cols: pl.BlockSpec((1, tile, cols), lambda b, h, t: (h, b * tiles + t, 0))
    return pl.pallas_call(
        functools.partial(_mixer_kernel, dv=dv),
        grid=(batch, heads, tiles),
        in_specs=[block(width), _resident((1, dv))],
        out_specs=block(dv),
        out_shape=jax.ShapeDtypeStruct((heads, n, dv), BF16),
        scratch_shapes=[pltpu.VMEM((LANES, dv), F32), pltpu.VMEM((tile // CHUNK, LANES, dv), BF16)],
        compiler_params=_params(("parallel", "parallel", "arbitrary")),
        name=f"mixer_dv{dv}",
    )(slab, out_gain.reshape(1, dv))


def _attn_mlp_kernel(x_ref, ymix_ref, xq_ref, kv_ref, wout_ref, gain_ref, wup_ref, wdn_ref,
                     fgain_ref, o_ref, *, final):
    y_mix = jnp.concatenate([ymix_ref[h] for h in range(ymix_ref.shape[0])], axis=1)
    acc = x_ref[...] + jnp.dot(y_mix, wout_ref[0:D_MODEL, :], preferred_element_type=F32)
    xq = xq_ref[...]
    kv = kv_ref[0]
    heads = []
    for h in range(XA_HEADS):
        lo, hi = h * XA_DH, (h + 1) * XA_DH
        s = lax.dot_general(xq[:, lo:hi], kv[:, lo:hi], _NT, preferred_element_type=F32)
        s = s * (XA_DH ** -0.5)
        e = jnp.exp(s - jnp.max(s, axis=-1, keepdims=True)).astype(BF16)
        vh = kv[:, XA_DIM + lo:XA_DIM + hi]
        num_den = jnp.dot(e, jnp.concatenate([vh, jnp.ones_like(vh)], axis=1),
                          preferred_element_type=F32)
        heads.append((num_den[:, :XA_DH] / num_den[:, XA_DH:]).astype(BF16))
    y_mem = jnp.concatenate(heads, axis=1)
    x = acc + jnp.dot(y_mem, wout_ref[D_MODEL:, :], preferred_element_type=F32)

    h = _rms_normed(x, gain_ref[...]).astype(BF16)
    mlp = None
    for c in range(0, MLP_HIDDEN, MLP_HIDDEN_TILE):
        u = jnp.maximum(jnp.dot(h, wup_ref[:, c:c + MLP_HIDDEN_TILE], preferred_element_type=F32), 0.0)
        d = jnp.dot((u * u).astype(BF16), wdn_ref[c:c + MLP_HIDDEN_TILE, :],
                    preferred_element_type=F32)
        mlp = d if mlp is None else mlp + d
    out = x + mlp
    o_ref[...] = _rms_normed(out, fgain_ref[...]) if final else out


def _attn_mlp(x2d, y_mix, xq, kv, w_out, gain, w_up, w_down, final_gain, layer, *, batch, final):
    n = x2d.shape[0]
    tiles = n // batch // ATTN_TILE
    row = lambda width: pl.BlockSpec((ATTN_TILE, width), lambda i: (i, 0))
    heads, _, dv = y_mix.shape
    return pl.pallas_call(
        functools.partial(_attn_mlp_kernel, final=final),
        grid=(n // ATTN_TILE,),
        in_specs=[row(D_MODEL), pl.BlockSpec((heads, ATTN_TILE, dv), lambda i: (0, i, 0)),
                  row(XA_DIM),
                  pl.BlockSpec((1, MEM_LEN, 2 * XA_DIM), lambda i: (layer * batch + i // tiles, 0, 0)),
                  _resident((D_MODEL + XA_DIM, D_MODEL)), _resident((1, D_MODEL)),
                  _resident((D_MODEL, MLP_HIDDEN)), _resident((MLP_HIDDEN, D_MODEL)),
                  _resident((1, D_MODEL))],
        out_specs=row(D_MODEL),
        out_shape=jax.ShapeDtypeStruct((n, D_MODEL), F32),
        compiler_params=_params(("parallel",)),
        name="attn_mlp_final" if final else "attn_mlp",
    )(x2d, y_mix, xq, kv, w_out, gain.reshape(1, D_MODEL), w_up, w_down,
      final_gain.reshape(1, D_MODEL))


def kernel(x, mem, norm_mix, norm_mem, w_kv, w_out, norm_mlp, w_up, w_down,
           gla_w_in, gla_w_gate2, gla_b_gate, gla_out_gain,
           hgrn_w_in, hgrn_lower_bounds, hgrn_out_gain, final_norm):
    batch, seq, _ = x.shape
    assert seq % ROW_TILE == 0 and seq % ATTN_TILE == 0
    x2d = x.reshape(batch * seq, D_MODEL)
    w_kv, w_out, w_up, w_down, gla_w_in, gla_w_gate2, hgrn_w_in = (
        a.astype(BF16) for a in (w_kv, w_out, w_up, w_down, gla_w_in, gla_w_gate2, hgrn_w_in))
    kv = _memkv(mem.reshape(batch * MEM_LEN, D_MODEL), norm_mem, w_kv)
    kv = kv.reshape(DEPTH * batch, MEM_LEN, 2 * XA_DIM)

    for i in range(DEPTH):
        j = i // 2
        if i % 2 == 0:
            slab, xq = _proj_gla(x2d, norm_mix[i], gla_w_in[j], gla_w_gate2[j], gla_b_gate[j])
            y_mix = _mixer(slab, gla_out_gain[j], batch=batch)
        else:
            slab, xq = _proj_hgrn(x2d, norm_mix[i], hgrn_w_in[j], hgrn_lower_bounds, i)
            y_mix = _mixer(slab, hgrn_out_gain[j], batch=batch)
        x2d = _attn_mlp(x2d, y_mix, xq, kv, w_out[i], norm_mlp[i], w_up[i], w_down[i], final_norm,
                        i, batch=batch, final=(i == DEPTH - 1))
    return x2d.reshape(batch, seq, D_MODEL)
```
